```python
import jax, jax.numpy as jnp
from jax import lax
import numpy as np

D_MODEL = 1024
BATCH = 8
SEQ = 4096
DEPTH = 2

LRU_HEADS = 16
LRU_HEAD_DIM = 64
D_LRU = LRU_HEADS * LRU_HEAD_DIM
SC_GROUPS = 8
SC_GROUP_DIM = 64
D_SC = SC_GROUPS * SC_GROUP_DIM
D_MIX = D_LRU + D_SC
D_IN = 2 * D_LRU + 3 * D_SC
LRU_CONV_WIDTH = 4
SC_CONV_WIDTH = 3
RG_C = 8.0
D_FF = 3 * D_MODEL
FFN_CONV_WIDTH = 3
EPS = 1e-6

kernel_name = "hymba_style_rglru_shortconv_convffn"


def rms_norm(x, g):
    xf = x.astype(jnp.float32)
    y = xf * lax.rsqrt(jnp.mean(xf * xf, axis=-1, keepdims=True) + EPS)
    return (y * g.astype(jnp.float32)).astype(x.dtype)


def causal_dwconv(x, w):
    k_width = w.shape[0]
    s = x.shape[1]
    xp = jnp.pad(x, ((0, 0), (k_width - 1, 0), (0, 0)))
    y = xp[:, 0:s] * w[0]
    for k in range(1, k_width):
        y = y + xp[:, k:k + s] * w[k]
    return y


def rg_lru(x, wa, ba, wx, bx, lam):
    bsz, s, c = x.shape
    xh = x.reshape(bsz, s, LRU_HEADS, LRU_HEAD_DIM)
    r = jax.nn.sigmoid(jnp.einsum('bshi,hij->bshj', xh, wa).reshape(bsz, s, c) + ba)
    i = jax.nn.sigmoid(jnp.einsum('bshi,hij->bshj', xh, wx).reshape(bsz, s, c) + bx)
    log_a = -RG_C * r.astype(jnp.float32) * jax.nn.softplus(-lam.astype(jnp.float32))
    a = jnp.exp(log_a)
    mult = jnp.sqrt(-jnp.expm1(2.0 * log_a))
    b = mult * (i * x).astype(jnp.float32)

    def combine(left, right):
        a1, b1 = left
        a2, b2 = right
        return a1 * a2, a2 * b1 + b2

    _, h = lax.associative_scan(combine, (a, b), axis=1)
    return h.astype(x.dtype)


def setup_inputs(seed: int = 0) -> dict:
    key = jax.random.key(seed)
    ks = jax.random.split(key, 20)
    f32 = jnp.float32
    res_scale = (2.0 * DEPTH) ** -0.5
    x = jax.random.normal(ks[0], (BATCH, SEQ, D_MODEL), f32)
    norm1_g = 1.0 + 0.02 * jax.random.normal(ks[1], (DEPTH, D_MODEL), f32)
    w_in = jax.random.normal(ks[2], (DEPTH, D_MODEL, D_IN), f32) * D_MODEL ** -0.5
    lru_conv_w = jax.random.normal(ks[3], (DEPTH, LRU_CONV_WIDTH, D_LRU), f32) * LRU_CONV_WIDTH ** -0.5
    lru_conv_b = 0.02 * jax.random.normal(ks[4], (DEPTH, D_LRU), f32)
    lru_wa = jax.random.normal(ks[5], (DEPTH, LRU_HEADS, LRU_HEAD_DIM, LRU_HEAD_DIM), f32) * LRU_HEAD_DIM ** -0.5
    lru_ba = 0.02 * jax.random.normal(ks[6], (DEPTH, D_LRU), f32)
    lru_wx = jax.random.normal(ks[7], (DEPTH, LRU_HEADS, LRU_HEAD_DIM, LRU_HEAD_DIM), f32) * LRU_HEAD_DIM ** -0.5
    lru_bx = 0.02 * jax.random.normal(ks[8], (DEPTH, D_LRU), f32)
    u = jax.random.uniform(ks[9], (DEPTH, D_LRU), f32, minval=0.9, maxval=0.999)
    a0 = u ** (1.0 / RG_C)
    lru_lambda = jnp.log(a0) - jnp.log1p(-a0)
    sc_conv_w = jax.random.normal(ks[10], (DEPTH, SC_CONV_WIDTH, D_SC), f32) * SC_CONV_WIDTH ** -0.5
    w_out = jax.random.normal(ks[11], (DEPTH, D_MIX, D_MODEL), f32) * D_MIX ** -0.5 * res_scale
    norm2_g = 1.0 + 0.02 * jax.random.normal(ks[12], (DEPTH, D_MODEL), f32)
    w_up = jax.random.normal(ks[13], (DEPTH, D_MODEL, 2 * D_FF), f32) * D_MODEL ** -0.5
    ffn_conv_w = jax.random.normal(ks[14], (DEPTH, FFN_CONV_WIDTH, 2 * D_FF), f32) * FFN_CONV_WIDTH ** -0.5
    w_down = jax.random.normal(ks[15], (DEPTH, D_FF, D_MODEL), f32) * D_FF ** -0.5 * res_scale
    final_g = 1.0 + 0.02 * jax.random.normal(ks[16], (D_MODEL,), f32)
    return {"x": x, "norm1_g": norm1_g, "w_in": w_in, "lru_conv_w": lru_conv_w,
            "lru_conv_b": lru_conv_b, "lru_wa": lru_wa, "lru_ba": lru_ba, "lru_wx": lru_wx,
            "lru_bx": lru_bx, "lru_lambda": lru_lambda, "sc_conv_w": sc_conv_w, "w_out": w_out,
            "norm2_g": norm2_g, "w_up": w_up, "ffn_conv_w": ffn_conv_w, "w_down": w_down,
            "final_g": final_g}


def reference(x, norm1_g, w_in, lru_conv_w, lru_conv_b, lru_wa, lru_ba, lru_wx, lru_bx,
              lru_lambda, sc_conv_w, w_out, norm2_g, w_up, ffn_conv_w, w_down, final_g):
    splits = [D_LRU, 2 * D_LRU, 2 * D_LRU + D_SC, 2 * D_LRU + 2 * D_SC]
    for l in range(DEPTH):
        h = rms_norm(x, norm1_g[l])
        z = jnp.einsum('bsd,de->bse', h, w_in[l])
        lru_x, lru_gate, sc_b, sc_c, sc_x = jnp.split(z, splits, axis=-1)
        lru_x = causal_dwconv(lru_x, lru_conv_w[l]) + lru_conv_b[l]
        y_lru = rg_lru(lru_x, lru_wa[l], lru_ba[l], lru_wx[l], lru_bx[l], lru_lambda[l]) \
            * jax.nn.gelu(lru_gate)
        y_sc = sc_b * causal_dwconv(sc_c * sc_x, sc_conv_w[l])
        y_mix = jnp.concatenate([y_lru, y_sc], axis=-1)
        x = x + jnp.einsum('bse,ed->bsd', y_mix, w_out[l])
        h = rms_norm(x, norm2_g[l])
        u = causal_dwconv(jnp.einsum('bsd,df->bsf', h, w_up[l]), ffn_conv_w[l])
        gate, up = jnp.split(u, 2, axis=-1)
        x = x + jnp.einsum('bsf,fd->bsd', jax.nn.gelu(gate) * up, w_down[l])
    return rms_norm(x, final_g)
```

```python
import functools
import math

import jax
import jax.numpy as jnp
from jax import lax
from jax.experimental import pallas as pl
from jax.experimental.pallas import tpu as pltpu

EPS = 1e-6
RG_C = 8.0
LRU_HEAD_DIM = 64
LRU_CONV_WIDTH = 4
SC_CONV_WIDTH = 3
FFN_CONV_WIDTH = 3

V7X_SUBLANES = 8
V7X_MXU_DIM = 256
V7X_VMEM_BYTES = 64 * 1024 * 1024
VMEM_LIMIT_BYTES = V7X_VMEM_BYTES - 8 * 1024 * 1024

ROWS_PER_STEP = 256
FFN_CHUNK = 512


def _rms_norm(x, g):
    ms = jnp.mean(x * x, axis=-1, keepdims=True)
    return x * lax.rsqrt(ms + EPS) * g


def _gelu_tanh(x):
    c = math.sqrt(2.0 / math.pi)
    return 0.5 * x * (1.0 + jnp.tanh(c * (x + 0.044715 * (x * x * x))))


def _sigmoid(x):
    return 0.5 * (jnp.tanh(0.5 * x) + 1.0)


def _softplus(x):
    return jnp.maximum(x, 0.0) + jnp.log1p(jnp.exp(-jnp.abs(x)))


def _causal_conv(ext_ref, w_ref, col0, ncols, rows, batch, width):
    acc = None
    for k in range(width):
        term = w_ref[k:k + 1, col0:col0 + ncols] * ext_ref[k * batch:k * batch + rows, col0:col0 + ncols]
        acc = term if acc is None else acc + term
    return acc


def _mixer_kernel(x_ref, g_ref, w_in_ref, cw_ref, cb_ref, wg_ref, ba_ref, bx_ref, lam_ref,
                  scw_ref, w_out_ref, o_ref,
                  zx_ext, v_ext, a_s, b_s, h_carry, ymix, *, batch, d_lru, d_sc):
    rows = x_ref.shape[0]
    lru_halo = (LRU_CONV_WIDTH - 1) * batch
    sc_halo = (SC_CONV_WIDTH - 1) * batch

    @pl.when(pl.program_id(0) == 0)
    def _():
        zx_ext[0:lru_halo, :] = jnp.zeros((lru_halo, d_lru), jnp.float32)
        v_ext[0:sc_halo, :] = jnp.zeros((sc_halo, d_sc), jnp.float32)
        h_carry[...] = jnp.zeros_like(h_carry)

    x = x_ref[...]
    hb = _rms_norm(x, g_ref[...]).astype(jnp.bfloat16)

    def proj(c0, n):
        return jnp.dot(hb, w_in_ref[:, c0:c0 + n], preferred_element_type=jnp.float32)

    zx_ext[lru_halo:, :] = proj(0, d_lru)
    lru_x = _causal_conv(zx_ext, cw_ref, 0, d_lru, rows, batch, LRU_CONV_WIDTH) + cb_ref[...]
    zx_ext[0:lru_halo, :] = zx_ext[rows:rows + lru_halo, :]

    xb = lru_x.astype(jnp.bfloat16)
    n_groups = d_lru // V7X_MXU_DIM
    r_parts, i_parts = [], []
    for g in range(n_groups):
        rg = jnp.dot(xb[:, g * V7X_MXU_DIM:(g + 1) * V7X_MXU_DIM], wg_ref[g],
                     preferred_element_type=jnp.float32)
        r_parts.append(rg[:, :V7X_MXU_DIM])
        i_parts.append(rg[:, V7X_MXU_DIM:])
    r = _sigmoid(jnp.concatenate(r_parts, axis=1) + ba_ref[...])
    ig = _sigmoid(jnp.concatenate(i_parts, axis=1) + bx_ref[...])
    log_a = r * (-RG_C * _softplus(-lam_ref[...]))
    a = jnp.exp(log_a)
    mult = jnp.sqrt(-jnp.tanh(log_a) * (a * a + 1.0))
    a_s[...] = a
    b_s[...] = mult * (ig * lru_x)

    def scan_step(t, h):
        r0 = pl.multiple_of(t * batch, batch)
        h = a_s[pl.ds(r0, batch), :] * h + b_s[pl.ds(r0, batch), :]
        b_s[pl.ds(r0, batch), :] = h
        return h

    h_carry[...] = lax.fori_loop(0, rows // batch, scan_step, h_carry[...], unroll=8)
    y_lru = b_s[...] * _gelu_tanh(proj(d_lru, d_lru))
    ymix[:, 0:d_lru] = y_lru.astype(jnp.bfloat16)

    sc_b = proj(2 * d_lru, d_sc)
    sc_c = proj(2 * d_lru + d_sc, d_sc)
    sc_x = proj(2 * d_lru + 2 * d_sc, d_sc)
    v_ext[sc_halo:, :] = sc_c * sc_x
    y_sc = sc_b * _causal_conv(v_ext, scw_ref, 0, d_sc, rows, batch, SC_CONV_WIDTH)
    v_ext[0:sc_halo, :] = v_ext[rows:rows + sc_halo, :]
    ymix[:, d_lru:] = y_sc.astype(jnp.bfloat16)

    o_ref[...] = x + jnp.dot(ymix[...], w_out_ref[...], preferred_element_type=jnp.float32)


def _ffn_kernel(x_ref, g_ref, w_up_ref, cw_ref, w_down_ref, fg_ref, o_ref, u_ext,
                *, batch, d_ff, final_norm):
    rows = x_ref.shape[0]
    halo = (FFN_CONV_WIDTH - 1) * batch

    @pl.when(pl.program_id(0) == 0)
    def _():
        u_ext[0:halo, :] = jnp.zeros((halo, 2 * d_ff), jnp.float32)

    x = x_ref[...]
    hb = _rms_norm(x, g_ref[...]).astype(jnp.bfloat16)
    acc = x
    for c in range(d_ff // FFN_CHUNK):
        cols = (c * FFN_CHUNK, d_ff + c * FFN_CHUNK)
        convs = []
        for c0 in cols:
            u_ext[halo:, c0:c0 + FFN_CHUNK] = jnp.dot(
                hb, w_up_ref[:, c0:c0 + FFN_CHUNK], preferred_element_type=jnp.float32)
            convs.append(_causal_conv(u_ext, cw_ref, c0, FFN_CHUNK, rows, batch, FFN_CONV_WIDTH))
            u_ext[0:halo, c0:c0 + FFN_CHUNK] = u_ext[rows:rows + halo, c0:c0 + FFN_CHUNK]
        act = (_gelu_tanh(convs[0]) * convs[1]).astype(jnp.bfloat16)
        acc = acc + jnp.dot(act, w_down_ref[c * FFN_CHUNK:(c + 1) * FFN_CHUNK, :],
                            preferred_element_type=jnp.float32)
    if final_norm:
        acc = _rms_norm(acc, fg_ref[...])
    o_ref[...] = acc


def _resident(shape):
    return pl.BlockSpec(shape, lambda i: (0,) * len(shape), pipeline_mode=pl.Buffered(1))


def _compiler_params():
    return pltpu.CompilerParams(dimension_semantics=("arbitrary",),
                                vmem_limit_bytes=VMEM_LIMIT_BYTES)


def _mixer_call(xt, g, w_in, cw, cb, wg, ba, bx, lam, scw, w_out, *, batch):
    n_rows, d_model = xt.shape
    d_lru = cw.shape[1]
    d_sc = scw.shape[1]
    rows = ROWS_PER_STEP
    row_spec = pl.BlockSpec((rows, d_model), lambda i: (i, 0))
    operands = (g, w_in, cw, cb, wg, ba, bx, lam, scw, w_out)
    return pl.pallas_call(
        functools.partial(_mixer_kernel, batch=batch, d_lru=d_lru, d_sc=d_sc),
        grid=(n_rows // rows,),
        in_specs=[row_spec] + [_resident(a.shape) for a in operands],
        out_specs=row_spec,
        out_shape=jax.ShapeDtypeStruct(xt.shape, xt.dtype),
        scratch_shapes=[
            pltpu.VMEM((rows + (LRU_CONV_WIDTH - 1) * batch, d_lru), jnp.float32),
            pltpu.VMEM((rows + (SC_CONV_WIDTH - 1) * batch, d_sc), jnp.float32),
            pltpu.VMEM((rows, d_lru), jnp.float32),
            pltpu.VMEM((rows, d_lru), jnp.float32),
            pltpu.VMEM((batch, d_lru), jnp.float32),
            pltpu.VMEM((rows, d_lru + d_sc), jnp.bfloat16),
        ],
        compiler_params=_compiler_params(),
        name="mixer",
    )(xt, *operands)


def _ffn_call(xt, g, w_up, cw, w_down, fg, *, batch, final_norm):
    n_rows, d_model = xt.shape
    d_ff = w_down.shape[0]
    rows = ROWS_PER_STEP
    row_spec = pl.BlockSpec((rows, d_model), lambda i: (i, 0))
    operands = (g, w_up, cw, w_down, fg)
    return pl.pallas_call(
        functools.partial(_ffn_kernel, batch=batch, d_ff=d_ff, final_norm=final_norm),
        grid=(n_rows // rows,),
        in_specs=[row_spec] + [_resident(a.shape) for a in operands],
        out_specs=row_spec,
        out_shape=jax.ShapeDtypeStruct(xt.shape, xt.dtype),
        scratch_shapes=[pltpu.VMEM((rows + (FFN_CONV_WIDTH - 1) * batch, 2 * d_ff), jnp.float32)],
        compiler_params=_compiler_params(),
        name="convffn",
    )(xt, *operands)


def _block_diag_gates(wa, wx):
    heads, hd, _ = wa.shape
    per = V7X_MXU_DIM // hd
    eye = jnp.eye(per, dtype=wa.dtype)

    def bd(w):
        w = w.reshape(heads // per, per, hd, hd)
        return jnp.einsum('gjio,jk->gjiko', w, eye).reshape(heads // per, V7X_MXU_DIM, V7X_MXU_DIM)

    return jnp.concatenate([bd(wa), bd(wx)], axis=-1).astype(jnp.bfloat16)


def kernel(x, norm1_g, w_in, lru_conv_w, lru_conv_b, lru_wa, lru_ba, lru_wx, lru_bx, lru_lambda,
           sc_conv_w, w_out, norm2_g, w_up, ffn_conv_w, w_down, final_g):
    bsz, seq, d_model = x.shape
    depth = w_in.shape[0]
    assert bsz == V7X_SUBLANES and (seq * bsz) % ROWS_PER_STEP == 0
    bf16 = jnp.bfloat16
    row = lambda v: v.reshape(1, -1)
    xt = jnp.transpose(x, (1, 0, 2)).reshape(seq * bsz, d_model)
    for l in range(depth):
        xt = _mixer_call(
            xt, row(norm1_g[l]), w_in[l].astype(bf16), lru_conv_w[l], row(lru_conv_b[l]),
            _block_diag_gates(lru_wa[l], lru_wx[l]), row(lru_ba[l]), row(lru_bx[l]),
            row(lru_lambda[l]), sc_conv_w[l], w_out[l].astype(bf16), batch=bsz)
        xt = _ffn_call(
            xt, row(norm2_g[l]), w_up[l].astype(bf16), ffn_conv_w[l], w_down[l].astype(bf16),
            row(final_g), batch=bsz, final_norm=(l == depth - 1))
    return jnp.transpose(xt.reshape(seq, bsz, d_model), (1, 0, 2))
```

```python
import functools
import math

import jax
import jax.numpy as jnp
from jax import lax
from jax.experimental import pallas as pl
from jax.experimental.pallas import tpu as pltpu

EPS = 1e-6
RG_C = 8.0
LRU_CONV_WIDTH = 4
SC_CONV_WIDTH = 3
FFN_CONV_WIDTH = 3

V7X_SUBLANES = 8
V7X_MXU_DIM = 256
V7X_VMEM_BYTES = 64 * 1024 * 1024
VMEM_LIMIT_BYTES = V7X_VMEM_BYTES - 8 * 1024 * 1024

MIXER_ROWS = 256
FFN_ROWS = 512
FFN_CHUNK = 1024
BF16_PER_WORD = 2


def _rms_norm(x, g):
    ms = jnp.mean(x * x, axis=-1, keepdims=True)
    return x * lax.rsqrt(ms + EPS) * g


def _gelu_tanh(x):
    c = math.sqrt(2.0 / math.pi)
    return 0.5 * x * (1.0 + jnp.tanh(c * (x + 0.044715 * (x * x * x))))


def _sigmoid(x):
    return 0.5 * (jnp.tanh(0.5 * x) + 1.0)


def _softplus(x):
    return jnp.maximum(x, 0.0) + jnp.log1p(jnp.exp(-jnp.abs(x)))


def _weight(w_ref, r0, nrows, c0, ncols):
    p = BF16_PER_WORD
    return pltpu.bitcast(w_ref[r0 // p:(r0 + nrows) // p, c0:c0 + ncols], jnp.bfloat16)


def _causal_conv(ext_ref, w_ref, col0, ncols, rows, batch, width):
    acc = None
    for k in range(width):
        term = w_ref[k:k + 1, col0:col0 + ncols] * ext_ref[k * batch:k * batch + rows, col0:col0 + ncols]
        acc = term if acc is None else acc + term
    return acc


def _mixer_kernel(x_ref, g_ref, w_in_ref, cw_ref, cb_ref, wg_ref, ba_ref, bx_ref, lam_ref,
                  scw_ref, w_out_ref, o_ref,
                  zx_ext, v_ext, a_s, b_s, h_carry, ymix, *, batch, d_lru, d_sc):
    rows, d_model = x_ref.shape
    lru_halo = (LRU_CONV_WIDTH - 1) * batch
    sc_halo = (SC_CONV_WIDTH - 1) * batch

    @pl.when(pl.program_id(0) == 0)
    def _():
        zx_ext[0:lru_halo, :] = jnp.zeros((lru_halo, d_lru), jnp.float32)
        v_ext[0:sc_halo, :] = jnp.zeros((sc_halo, d_sc), jnp.float32)
        h_carry[...] = jnp.zeros_like(h_carry)

    x = x_ref[...]
    hb = _rms_norm(x, g_ref[...]).astype(jnp.bfloat16)

    def proj(c0, n):
        return jnp.dot(hb, _weight(w_in_ref, 0, d_model, c0, n), preferred_element_type=jnp.float32)

    zx_ext[lru_halo:, :] = proj(0, d_lru)
    lru_x = _causal_conv(zx_ext, cw_ref, 0, d_lru, rows, batch, LRU_CONV_WIDTH) + cb_ref[...]
    zx_ext[0:lru_halo, :] = zx_ext[rows:rows + lru_halo, :]

    xb = lru_x.astype(jnp.bfloat16)
    gdim = V7X_MXU_DIM
    r_parts, i_parts = [], []
    for g in range(d_lru // gdim):
        rg = jnp.dot(xb[:, g * gdim:(g + 1) * gdim], pltpu.bitcast(wg_ref[g], jnp.bfloat16),
                     preferred_element_type=jnp.float32)
        r_parts.append(rg[:, :gdim])
        i_parts.append(rg[:, gdim:])
    r = _sigmoid(jnp.concatenate(r_parts, axis=1) + ba_ref[...])
    ig = _sigmoid(jnp.concatenate(i_parts, axis=1) + bx_ref[...])
    log_a = r * (-RG_C * _softplus(-lam_ref[...]))
    a = jnp.exp(log_a)
    mult = jnp.sqrt(-jnp.tanh(log_a) * (a * a + 1.0))
    a_s[...] = a
    b_s[...] = mult * (ig * lru_x)

    h = h_carry[...]
    for t in range(rows // batch):
        r0 = t * batch
        h = a_s[r0:r0 + batch, :] * h + b_s[r0:r0 + batch, :]
        b_s[r0:r0 + batch, :] = h
    h_carry[...] = h
    y_lru = b_s[...] * _gelu_tanh(proj(d_lru, d_lru))
    ymix[:, 0:d_lru] = y_lru.astype(jnp.bfloat16)

    sc_b = proj(2 * d_lru, d_sc)
    sc_c = proj(2 * d_lru + d_sc, d_sc)
    sc_x = proj(2 * d_lru + 2 * d_sc, d_sc)
    v_ext[sc_halo:, :] = sc_c * sc_x
    y_sc = sc_b * _causal_conv(v_ext, scw_ref, 0, d_sc, rows, batch, SC_CONV_WIDTH)
    v_ext[0:sc_halo, :] = v_ext[rows:rows + sc_halo, :]
    ymix[:, d_lru:] = y_sc.astype(jnp.bfloat16)

    w_out = _weight(w_out_ref, 0, d_lru + d_sc, 0, d_model)
    o_ref[...] = x + jnp.dot(ymix[...], w_out, preferred_element_type=jnp.float32)


def _ffn_kernel(x_ref, g_ref, w_up_ref, cw_ref, w_down_ref, fg_ref, o_ref, u_ext,
                *, batch, d_ff, final_norm):
    rows, d_model = x_ref.shape
    halo = (FFN_CONV_WIDTH - 1) * batch
    n_chunks = d_ff // FFN_CHUNK

    @pl.when(pl.program_id(0) == 0)
    def _():
        u_ext[0:halo, :] = jnp.zeros((halo, 2 * d_ff), jnp.float32)

    x = x_ref[...]
    hb = _rms_norm(x, g_ref[...]).astype(jnp.bfloat16)

    def chunk_cols(c):
        return (c * FFN_CHUNK, d_ff + c * FFN_CHUNK)

    def up_project(c):
        for c0 in chunk_cols(c):
            u_ext[halo:, c0:c0 + FFN_CHUNK] = jnp.dot(
                hb, _weight(w_up_ref, 0, d_model, c0, FFN_CHUNK), preferred_element_type=jnp.float32)

    def conv(c0):
        y = _causal_conv(u_ext, cw_ref, c0, FFN_CHUNK, rows, batch, FFN_CONV_WIDTH)
        u_ext[0:halo, c0:c0 + FFN_CHUNK] = u_ext[rows:rows + halo, c0:c0 + FFN_CHUNK]
        return y

    acc = x
    up_project(0)
    for c in range(n_chunks):
        if c + 1 < n_chunks:
            up_project(c + 1)
        gate_c0, up_c0 = chunk_cols(c)
        act = (_gelu_tanh(conv(gate_c0)) * conv(up_c0)).astype(jnp.bfloat16)
        acc = acc + jnp.dot(act, _weight(w_down_ref, c * FFN_CHUNK, FFN_CHUNK, 0, d_model),
                            preferred_element_type=jnp.float32)
    if final_norm:
        acc = _rms_norm(acc, fg_ref[...])
    o_ref[...] = acc


def _resident(shape):
    return pl.BlockSpec(shape, lambda i: (0,) * len(shape), pipeline_mode=pl.Buffered(1))


def _compiler_params():
    return pltpu.CompilerParams(dimension_semantics=("arbitrary",),
                                vmem_limit_bytes=VMEM_LIMIT_BYTES)


def _mixer_call(xt, g, w_in, cw, cb, wg, ba, bx, lam, scw, w_out, *, batch):
    n_rows, d_model = xt.shape
    d_lru = cw.shape[1]
    d_sc = scw.shape[1]
    rows = MIXER_ROWS
    row_spec = pl.BlockSpec((rows, d_model), lambda i: (i, 0))
    operands = (g, w_in, cw, cb, wg, ba, bx, lam, scw, w_out)
    return pl.pallas_call(
        functools.partial(_mixer_kernel, batch=batch, d_lru=d_lru, d_sc=d_sc),
        grid=(n_rows // rows,),
        in_specs=[row_spec] + [_resident(a.shape) for a in operands],
        out_specs=row_spec,
        out_shape=jax.ShapeDtypeStruct(xt.shape, xt.dtype),
        scratch_shapes=[
            pltpu.VMEM((rows + (LRU_CONV_WIDTH - 1) * batch, d_lru), jnp.float32),
            pltpu.VMEM((rows + (SC_CONV_WIDTH - 1) * batch, d_sc), jnp.float32),
            pltpu.VMEM((rows, d_lru), jnp.float32),
            pltpu.VMEM((rows, d_lru), jnp.float32),
            pltpu.VMEM((batch, d_lru), jnp.float32),
            pltpu.VMEM((rows, d_lru + d_sc), jnp.bfloat16),
        ],
        compiler_params=_compiler_params(),
        name="mixer",
    )(xt, *operands)


def _ffn_call(xt, g, w_up, cw, w_down, fg, *, batch, final_norm):
    n_rows, d_model = xt.shape
    d_ff = cw.shape[1] // 2
    rows = FFN_ROWS
    row_spec = pl.BlockSpec((rows, d_model), lambda i: (i, 0))
    operands = (g, w_up, cw, w_down, fg)
    return pl.pallas_call(
        functools.partial(_ffn_kernel, batch=batch, d_ff=d_ff, final_norm=final_norm),
        grid=(n_rows // rows,),
        in_specs=[row_spec] + [_resident(a.shape) for a in operands],
        out_specs=row_spec,
        out_shape=jax.ShapeDtypeStruct(xt.shape, xt.dtype),
        scratch_shapes=[pltpu.VMEM((rows + (FFN_CONV_WIDTH - 1) * batch, 2 * d_ff), jnp.float32)],
        compiler_params=_compiler_params(),
        name="convffn",
    )(xt, *operands)


def _pack_rows(w):
    *lead, k, n = w.shape
    wb = w.astype(jnp.bfloat16).reshape(*lead, k // BF16_PER_WORD, BF16_PER_WORD, n)
    return lax.bitcast_convert_type(jnp.swapaxes(wb, -1, -2), jnp.uint32)


def _block_diag_gates(wa, wx):
    heads, hd, _ = wa.shape
    per = V7X_MXU_DIM // hd
    eye = jnp.eye(per, dtype=wa.dtype)

    def bd(w):
        w = w.reshape(heads // per, per, hd, hd)
        return jnp.einsum('gjio,jk->gjiko', w, eye).reshape(heads // per, V7X_MXU_DIM, V7X_MXU_DIM)

    return jnp.concatenate([bd(wa), bd(wx)], axis=-1)


def kernel(x, norm1_g, w_in, lru_conv_w, lru_conv_b, lru_wa, lru_ba, lru_wx, lru_bx, lru_lambda,
           sc_conv_w, w_out, norm2_g, w_up, ffn_conv_w, w_down, final_g):
    bsz, seq, d_model = x.shape
    depth = w_in.shape[0]
    assert bsz == V7X_SUBLANES and (seq * bsz) % MIXER_ROWS == 0 and (seq * bsz) % FFN_ROWS == 0
    row = lambda v: v.reshape(1, -1)
    xt = jnp.transpose(x, (1, 0, 2)).reshape(seq * bsz, d_model)
    for l in range(depth):
        xt = _mixer_call(
            xt, row(norm1_g[l]), _pack_rows(w_in[l]), lru_conv_w[l], row(lru_conv_b[l]),
            _pack_rows(_block_diag_gates(lru_wa[l], lru_wx[l])), row(lru_ba[l]), row(lru_bx[l]),
            row(lru_lambda[l]), sc_conv_w[l], _pack_rows(w_out[l]), batch=bsz)
        xt = _ffn_call(
            xt, row(norm2_g[l]), _pack_rows(w_up[l]), ffn_conv_w[l], _pack_rows(w_down[l]),
            row(final_g), batch=bsz, final_norm=(l == depth - 1))
    return jnp.transpose(xt.reshape(seq, bsz, d_model), (1, 0, 2))
```

```python
import functools
import math

import jax
import jax.numpy as jnp
from jax import lax
from jax.experimental import pallas as pl
from jax.experimental.pallas import tpu as pltpu

EPS = 1e-6
RG_C = 8.0
LRU_CONV_WIDTH = 4
SC_CONV_WIDTH = 3
FFN_CONV_WIDTH = 3

V7X_SUBLANES = 8
V7X_MXU_DIM = 256
V7X_VMEM_BYTES = 64 * 1024 * 1024
VMEM_LIMIT_BYTES = V7X_VMEM_BYTES - 8 * 1024 * 1024

MIXER_ROWS = 256
FFN_ROWS = 512
FFN_CHUNK = 1024
BF16_PER_WORD = 2


def _rms_norm(x, g):
    ms = jnp.mean(x * x, axis=-1, keepdims=True)
    return x * lax.rsqrt(ms + EPS) * g


def _gelu_tanh(x):
    c = math.sqrt(2.0 / math.pi)
    return 0.5 * x * (1.0 + jnp.tanh(c * (x + 0.044715 * (x * x * x))))


def _sigmoid(x):
    return 0.5 * (jnp.tanh(0.5 * x) + 1.0)


def _softplus(x):
    return jnp.maximum(x, 0.0) + jnp.log1p(jnp.exp(-jnp.abs(x)))


def _weight(w_ref, r0, nrows, c0, ncols):
    p = BF16_PER_WORD
    return pltpu.bitcast(w_ref[r0 // p:(r0 + nrows) // p, c0:c0 + ncols], jnp.bfloat16)


def _causal_conv(ext_ref, w_ref, col0, ncols, rows, batch, width):
    acc = None
    for k in range(width):
        term = w_ref[k:k + 1, col0:col0 + ncols] * ext_ref[k * batch:k * batch + rows, col0:col0 + ncols]
        acc = term if acc is None else acc + term
    return acc


def _mixer_kernel(x_ref, g_ref, w_in_ref, cw_ref, cb_ref, wg_ref, ba_ref, bx_ref, lam_ref,
                  scw_ref, w_out_ref, o_ref,
                  zx_ext, v_ext, a_s, b_s, h_carry, ymix, *, batch, d_lru, d_sc):
    rows, d_model = x_ref.shape
    lru_halo = (LRU_CONV_WIDTH - 1) * batch
    sc_halo = (SC_CONV_WIDTH - 1) * batch

    @pl.when(pl.program_id(0) == 0)
    def _():
        zx_ext[0:lru_halo, :] = jnp.zeros((lru_halo, d_lru), jnp.float32)
        v_ext[0:sc_halo, :] = jnp.zeros((sc_halo, d_sc), jnp.float32)
        h_carry[...] = jnp.zeros_like(h_carry)

    x = x_ref[...]
    hb = _rms_norm(x, g_ref[...]).astype(jnp.bfloat16)

    def proj(c0, n):
        return jnp.dot(hb, _weight(w_in_ref, 0, d_model, c0, n), preferred_element_type=jnp.float32)

    zx_ext[lru_halo:, :] = proj(0, d_lru)
    lru_x = _causal_conv(zx_ext, cw_ref, 0, d_lru, rows, batch, LRU_CONV_WIDTH) + cb_ref[...]
    zx_ext[0:lru_halo, :] = zx_ext[rows:rows + lru_halo, :]

    xb = lru_x.astype(jnp.bfloat16)
    gdim = V7X_MXU_DIM
    r_parts, i_parts = [], []
    for g in range(d_lru // gdim):
        rg = jnp.dot(xb[:, g * gdim:(g + 1) * gdim], pltpu.bitcast(wg_ref[g], jnp.bfloat16),
                     preferred_element_type=jnp.float32)
        r_parts.append(rg[:, :gdim])
        i_parts.append(rg[:, gdim:])
    r = _sigmoid(jnp.concatenate(r_parts, axis=1) + ba_ref[...])
    ig = _sigmoid(jnp.concatenate(i_parts, axis=1) + bx_ref[...])
    log_a = r * (-RG_C * _softplus(-lam_ref[...]))
    a = jnp.exp(log_a)
    mult = jnp.sqrt(-jnp.tanh(log_a) * (a * a + 1.0))
    a_s[...] = a
    b_s[...] = mult * (ig * lru_x)

    h = h_carry[...]
    for t in range(rows // batch):
        r0 = t * batch
        h = a_s[r0:r0 + batch, :] * h + b_s[r0:r0 + batch, :]
        b_s[r0:r0 + batch, :] = h
    h_carry[...] = h
    y_lru = b_s[...] * _gelu_tanh(proj(d_lru, d_lru))
    ymix[:, 0:d_lru] = y_lru.astype(jnp.bfloat16)

    sc_b = proj(2 * d_lru, d_sc)
    sc_c = proj(2 * d_lru + d_sc, d_sc)
    sc_x = proj(2 * d_lru + 2 * d_sc, d_sc)
    v_ext[sc_halo:, :] = sc_c * sc_x
    y_sc = sc_b * _causal_conv(v_ext, scw_ref, 0, d_sc, rows, batch, SC_CONV_WIDTH)
    v_ext[0:sc_halo, :] = v_ext[rows:rows + sc_halo, :]
    ymix[:, d_lru:] = y_sc.astype(jnp.bfloat16)

    w_out = _weight(w_out_ref, 0, d_lru + d_sc, 0, d_model)
    o_ref[...] = x + jnp.dot(ymix[...], w_out, preferred_element_type=jnp.float32)


def _ffn_kernel(x_ref, g_ref, w_up_ref, cw_ref, w_down_ref, fg_ref, o_ref, u_ext,
                *, batch, d_ff, final_norm):
    rows, d_model = x_ref.shape
    halo = (FFN_CONV_WIDTH - 1) * batch
    n_chunks = d_ff // FFN_CHUNK

    @pl.when(pl.program_id(0) == 0)
    def _():
        u_ext[0:halo, :] = jnp.zeros((halo, 2 * d_ff), jnp.float32)

    x = x_ref[...]
    hb = _rms_norm(x, g_ref[...]).astype(jnp.bfloat16)

    def chunk_cols(c):
        return (c * FFN_CHUNK, d_ff + c * FFN_CHUNK)

    def up_project(c):
        for c0 in chunk_cols(c):
            u_ext[halo:, c0:c0 + FFN_CHUNK] = jnp.dot(
                hb, _weight(w_up_ref, 0, d_model, c0, FFN_CHUNK), preferred_element_type=jnp.float32)

    def conv(c0):
        y = _causal_conv(u_ext, cw_ref, c0, FFN_CHUNK, rows, batch, FFN_CONV_WIDTH)
        u_ext[0:halo, c0:c0 + FFN_CHUNK] = u_ext[rows:rows + halo, c0:c0 + FFN_CHUNK]
        return y

    acc = x
    up_project(0)
    for c in range(n_chunks):
        if c + 1 < n_chunks:
            up_project(c + 1)
        gate_c0, up_c0 = chunk_cols(c)
        act = (_gelu_tanh(conv(gate_c0)) * conv(up_c0)).astype(jnp.bfloat16)
        acc = acc + jnp.dot(act, _weight(w_down_ref, c * FFN_CHUNK, FFN_CHUNK, 0, d_model),
                            preferred_element_type=jnp.float32)
    if final_norm:
        acc = _rms_norm(acc, fg_ref[...])
    o_ref[...] = acc


def _resident(shape):
    return pl.BlockSpec(shape, lambda i: (0,) * len(shape), pipeline_mode=pl.Buffered(1))


def _compiler_params():
    return pltpu.CompilerParams(dimension_semantics=("arbitrary",),
                                vmem_limit_bytes=VMEM_LIMIT_BYTES)


def _mixer_call(xt, g, w_in, cw, cb, wg, ba, bx, lam, scw, w_out, *, batch):
    n_rows, d_model = xt.shape
    d_lru = cw.shape[1]
    d_sc = scw.shape[1]
    rows = MIXER_ROWS
    row_spec = pl.BlockSpec((rows, d_model), lambda i: (i, 0))
    operands = (g, w_in, cw, cb, wg, ba, bx, lam, scw, w_out)
    return pl.pallas_call(
        functools.partial(_mixer_kernel, batch=batch, d_lru=d_lru, d_sc=d_sc),
        grid=(n_rows // rows,),
        in_specs=[row_spec] + [_resident(a.shape) for a in operands],
        out_specs=row_spec,
        out_shape=jax.ShapeDtypeStruct(xt.shape, xt.dtype),
        scratch_shapes=[
            pltpu.VMEM((rows + (LRU_CONV_WIDTH - 1) * batch, d_lru), jnp.float32),
            pltpu.VMEM((rows + (SC_CONV_WIDTH - 1) * batch, d_sc), jnp.float32),
            pltpu.VMEM((rows, d_lru), jnp.float32),
            pltpu.VMEM((rows, d_lru), jnp.float32),
            pltpu.VMEM((batch, d_lru), jnp.float32),
            pltpu.VMEM((rows, d_lru + d_sc), jnp.bfloat16),
        ],
        compiler_params=_compiler_params(),
        name="mixer",
    )(xt, *operands)


def _ffn_call(xt, g, w_up, cw, w_down, fg, *, batch, final_norm):
    n_rows, d_model = xt.shape
    d_ff = cw.shape[1] // 2
    rows = FFN_ROWS
    row_spec = pl.BlockSpec((rows, d_model), lambda i: (i, 0))
    operands = (g, w_up, cw, w_down, fg)
    return pl.pallas_call(
        functools.partial(_ffn_kernel, batch=batch, d_ff=d_ff, final_norm=final_norm),
        grid=(n_rows // rows,),
        in_specs=[row_spec] + [_resident(a.shape) for a in operands],
        out_specs=row_spec,
        out_shape=jax.ShapeDtypeStruct(xt.shape, xt.dtype),
        scratch_shapes=[pltpu.VMEM((rows + (FFN_CONV_WIDTH - 1) * batch, 2 * d_ff), jnp.float32)],
        compiler_params=_compiler_params(),
        name="convffn",
    )(xt, *operands)


def _pack_rows(w):
    bits = lax.bitcast_convert_type(w.astype(jnp.bfloat16), jnp.uint16).astype(jnp.uint32)
    return bits[..., 0::BF16_PER_WORD, :] | (bits[..., 1::BF16_PER_WORD, :] << 16)


def _block_diag_gates(wa, wx):
    heads, hd, _ = wa.shape
    per = V7X_MXU_DIM // hd
    eye = jnp.eye(per, dtype=wa.dtype)

    def bd(w):
        w = w.reshape(heads // per, per, hd, hd)
        return jnp.einsum('gjio,jk->gjiko', w, eye).reshape(heads // per, V7X_MXU_DIM, V7X_MXU_DIM)

    return jnp.concatenate([bd(wa), bd(wx)], axis=-1)


def kernel(x, norm1_g, w_in, lru_conv_w, lru_conv_b, lru_wa, lru_ba, lru_wx, lru_bx, lru_lambda,
           sc_conv_w, w_out, norm2_g, w_up, ffn_conv_w, w_down, final_g):
    bsz, seq, d_model = x.shape
    depth = w_in.shape[0]
    assert bsz == V7X_SUBLANES and (seq * bsz) % MIXER_ROWS == 0 and (seq * bsz) % FFN_ROWS == 0
    row = lambda v: v.reshape(1, -1)
    xt = jnp.transpose(x, (1, 0, 2)).reshape(seq * bsz, d_model)
    for l in range(depth):
        xt = _mixer_call(
            xt, row(norm1_g[l]), _pack_rows(w_in[l]), lru_conv_w[l], row(lru_conv_b[l]),
            _pack_rows(_block_diag_gates(lru_wa[l], lru_wx[l])), row(lru_ba[l]), row(lru_bx[l]),
            row(lru_lambda[l]), sc_conv_w[l], _pack_rows(w_out[l]), batch=bsz)
        xt = _ffn_call(
            xt, row(norm2_g[l]), _pack_rows(w_up[l]), ffn_conv_w[l], _pack_rows(w_down[l]),
            row(final_g), batch=bsz, final_norm=(l == depth - 1))
    return jnp.transpose(xt.reshape(seq, bsz, d_model), (1, 0, 2))
```

```python
import functools
import math

import jax
import jax.numpy as jnp
from jax import lax
from jax.experimental import pallas as pl
from jax.experimental.pallas import tpu as pltpu

EPS = 1e-6
RG_C = 8.0
LRU_CONV_WIDTH = 4
SC_CONV_WIDTH = 3
FFN_CONV_WIDTH = 3

V7X_SUBLANES = 8
V7X_MXU_DIM = 256
V7X_VMEM_BYTES = 64 * 1024 * 1024
VMEM_LIMIT_BYTES = V7X_VMEM_BYTES - 8 * 1024 * 1024

MIXER_ROWS = 256
FFN_ROWS = 512
FFN_CHUNK = 1024
BF16_PER_WORD = 2
PACK_ROWS = 256


def _rms_norm(x, g):
    ms = jnp.mean(x * x, axis=-1, keepdims=True)
    return x * lax.rsqrt(ms + EPS) * g


def _gelu_tanh(x):
    c = math.sqrt(2.0 / math.pi)
    return 0.5 * x * (1.0 + jnp.tanh(c * (x + 0.044715 * (x * x * x))))


def _sigmoid(x):
    return 0.5 * (jnp.tanh(0.5 * x) + 1.0)


def _softplus(x):
    return jnp.maximum(x, 0.0) + jnp.log1p(jnp.exp(-jnp.abs(x)))


def _weight(w_ref, r0, nrows, c0, ncols):
    p = BF16_PER_WORD
    return pltpu.bitcast(w_ref[r0 // p:(r0 + nrows) // p, c0:c0 + ncols], jnp.bfloat16)


def _causal_conv(ext_ref, w_ref, col0, ncols, rows, batch, width):
    acc = None
    for k in range(width):
        term = w_ref[k:k + 1, col0:col0 + ncols] * ext_ref[k * batch:k * batch + rows, col0:col0 + ncols]
        acc = term if acc is None else acc + term
    return acc


def _mixer_kernel(x_ref, g_ref, w_in_ref, cw_ref, cb_ref, wg_ref, ba_ref, bx_ref, lam_ref,
                  scw_ref, w_out_ref, o_ref,
                  zx_ext, v_ext, a_s, b_s, h_carry, ymix, *, batch, d_lru, d_sc):
    rows, d_model = x_ref.shape
    lru_halo = (LRU_CONV_WIDTH - 1) * batch
    sc_halo = (SC_CONV_WIDTH - 1) * batch

    @pl.when(pl.program_id(0) == 0)
    def _():
        zx_ext[0:lru_halo, :] = jnp.zeros((lru_halo, d_lru), jnp.float32)
        v_ext[0:sc_halo, :] = jnp.zeros((sc_halo, d_sc), jnp.float32)
        h_carry[...] = jnp.zeros_like(h_carry)

    x = x_ref[...]
    hb = _rms_norm(x, g_ref[...]).astype(jnp.bfloat16)

    def proj(c0, n):
        return jnp.dot(hb, _weight(w_in_ref, 0, d_model, c0, n), preferred_element_type=jnp.float32)

    zx_ext[lru_halo:, :] = proj(0, d_lru)
    lru_x = _causal_conv(zx_ext, cw_ref, 0, d_lru, rows, batch, LRU_CONV_WIDTH) + cb_ref[...]
    zx_ext[0:lru_halo, :] = zx_ext[rows:rows + lru_halo, :]

    xb = lru_x.astype(jnp.bfloat16)
    gdim = V7X_MXU_DIM
    r_parts, i_parts = [], []
    for g in range(d_lru // gdim):
        rg = jnp.dot(xb[:, g * gdim:(g + 1) * gdim], _weight(wg_ref, g * gdim, gdim, 0, 2 * gdim),
                     preferred_element_type=jnp.float32)
        r_parts.append(rg[:, :gdim])
        i_parts.append(rg[:, gdim:])
    r = _sigmoid(jnp.concatenate(r_parts, axis=1) + ba_ref[...])
    ig = _sigmoid(jnp.concatenate(i_parts, axis=1) + bx_ref[...])
    log_a = r * (-RG_C * _softplus(-lam_ref[...]))
    a = jnp.exp(log_a)
    mult = jnp.sqrt(-jnp.tanh(log_a) * (a * a + 1.0))
    a_s[...] = a
    b_s[...] = mult * (ig * lru_x)

    h = h_carry[...]
    for t in range(rows // batch):
        r0 = t * batch
        h = a_s[r0:r0 + batch, :] * h + b_s[r0:r0 + batch, :]
        b_s[r0:r0 + batch, :] = h
    h_carry[...] = h
    y_lru = b_s[...] * _gelu_tanh(proj(d_lru, d_lru))
    ymix[:, 0:d_lru] = y_lru.astype(jnp.bfloat16)

    sc_b = proj(2 * d_lru, d_sc)
    sc_c = proj(2 * d_lru + d_sc, d_sc)
    sc_x = proj(2 * d_lru + 2 * d_sc, d_sc)
    v_ext[sc_halo:, :] = sc_c * sc_x
    y_sc = sc_b * _causal_conv(v_ext, scw_ref, 0, d_sc, rows, batch, SC_CONV_WIDTH)
    v_ext[0:sc_halo, :] = v_ext[rows:rows + sc_halo, :]
    ymix[:, d_lru:] = y_sc.astype(jnp.bfloat16)

    w_out = _weight(w_out_ref, 0, d_lru + d_sc, 0, d_model)
    o_ref[...] = x + jnp.dot(ymix[...], w_out, preferred_element_type=jnp.float32)


def _ffn_kernel(x_ref, g_ref, w_up_ref, cw_ref, w_down_ref, fg_ref, o_ref, u_ext,
                *, batch, d_ff, final_norm):
    rows, d_model = x_ref.shape
    halo = (FFN_CONV_WIDTH - 1) * batch
    n_chunks = d_ff // FFN_CHUNK

    @pl.when(pl.program_id(0) == 0)
    def _():
        u_ext[0:halo, :] = jnp.zeros((halo, 2 * d_ff), jnp.float32)

    x = x_ref[...]
    hb = _rms_norm(x, g_ref[...]).astype(jnp.bfloat16)

    def chunk_cols(c):
        return (c * FFN_CHUNK, d_ff + c * FFN_CHUNK)

    def up_project(c):
        for c0 in chunk_cols(c):
            u_ext[halo:, c0:c0 + FFN_CHUNK] = jnp.dot(
                hb, _weight(w_up_ref, 0, d_model, c0, FFN_CHUNK), preferred_element_type=jnp.float32)

    def conv(c0):
        y = _causal_conv(u_ext, cw_ref, c0, FFN_CHUNK, rows, batch, FFN_CONV_WIDTH)
        u_ext[0:halo, c0:c0 + FFN_CHUNK] = u_ext[rows:rows + halo, c0:c0 + FFN_CHUNK]
        return y

    acc = x
    up_project(0)
    for c in range(n_chunks):
        if c + 1 < n_chunks:
            up_project(c + 1)
        gate_c0, up_c0 = chunk_cols(c)
        act = (_gelu_tanh(conv(gate_c0)) * conv(up_c0)).astype(jnp.bfloat16)
        acc = acc + jnp.dot(act, _weight(w_down_ref, c * FFN_CHUNK, FFN_CHUNK, 0, d_model),
                            preferred_element_type=jnp.float32)
    if final_norm:
        acc = _rms_norm(acc, fg_ref[...])
    o_ref[...] = acc


def _resident(stacked, layer):
    _, a, b = stacked.shape
    return pl.BlockSpec((None, a, b), lambda i: (layer, 0, 0), pipeline_mode=pl.Buffered(1))


def _compiler_params():
    return pltpu.CompilerParams(dimension_semantics=("arbitrary",),
                                vmem_limit_bytes=VMEM_LIMIT_BYTES)


def _mixer_call(xt, layer, g, w_in, cw, cb, wg, ba, bx, lam, scw, w_out, *, batch):
    n_rows, d_model = xt.shape
    d_lru = cw.shape[-1]
    d_sc = scw.shape[-1]
    rows = MIXER_ROWS
    row_spec = pl.BlockSpec((rows, d_model), lambda i: (i, 0))
    operands = (g, w_in, cw, cb, wg, ba, bx, lam, scw, w_out)
    return pl.pallas_call(
        functools.partial(_mixer_kernel, batch=batch, d_lru=d_lru, d_sc=d_sc),
        grid=(n_rows // rows,),
        in_specs=[row_spec] + [_resident(a, layer) for a in operands],
        out_specs=row_spec,
        out_shape=jax.ShapeDtypeStruct(xt.shape, xt.dtype),
        scratch_shapes=[
            pltpu.VMEM((rows + (LRU_CONV_WIDTH - 1) * batch, d_lru), jnp.float32),
            pltpu.VMEM((rows + (SC_CONV_WIDTH - 1) * batch, d_sc), jnp.float32),
            pltpu.VMEM((rows, d_lru), jnp.float32),
            pltpu.VMEM((rows, d_lru), jnp.float32),
            pltpu.VMEM((batch, d_lru), jnp.float32),
            pltpu.VMEM((rows, d_lru + d_sc), jnp.bfloat16),
        ],
        compiler_params=_compiler_params(),
        name="mixer",
    )(xt, *operands)


def _ffn_call(xt, layer, g, w_up, cw, w_down, fg, *, batch, final_norm):
    n_rows, d_model = xt.shape
    d_ff = cw.shape[-1] // 2
    rows = FFN_ROWS
    row_spec = pl.BlockSpec((rows, d_model), lambda i: (i, 0))
    operands = (g, w_up, cw, w_down)
    return pl.pallas_call(
        functools.partial(_ffn_kernel, batch=batch, d_ff=d_ff, final_norm=final_norm),
        grid=(n_rows // rows,),
        in_specs=[row_spec] + [_resident(a, layer) for a in operands] + [_resident(fg, 0)],
        out_specs=row_spec,
        out_shape=jax.ShapeDtypeStruct(xt.shape, xt.dtype),
        scratch_shapes=[pltpu.VMEM((rows + (FFN_CONV_WIDTH - 1) * batch, 2 * d_ff), jnp.float32)],
        compiler_params=_compiler_params(),
        name="convffn",
    )(xt, *operands, fg)


def _pack_kernel(w_ref, o_ref):
    o_ref[...] = pltpu.bitcast(w_ref[...].astype(jnp.bfloat16), jnp.uint32)


def _pack_rows(w):
    layers, k, n = w.shape
    bk = min(k, PACK_ROWS)
    return pl.pallas_call(
        _pack_kernel,
        grid=(layers, k // bk),
        in_specs=[pl.BlockSpec((None, bk, n), lambda l, i: (l, i, 0))],
        out_specs=pl.BlockSpec((None, bk // BF16_PER_WORD, n), lambda l, i: (l, i, 0)),
        out_shape=jax.ShapeDtypeStruct((layers, k // BF16_PER_WORD, n), jnp.uint32),
        compiler_params=pltpu.CompilerParams(dimension_semantics=("arbitrary", "arbitrary"),
                                             vmem_limit_bytes=VMEM_LIMIT_BYTES),
        name="pack_weights",
    )(w)


def _block_diag_gates(wa, wx):
    layers, heads, hd, _ = wa.shape
    per = V7X_MXU_DIM // hd
    eye = jnp.eye(per, dtype=wa.dtype)

    def bd(w):
        w = w.reshape(layers, heads // per, per, hd, hd)
        return jnp.einsum('lgjio,jk->lgjiko', w, eye).reshape(layers, heads * hd, V7X_MXU_DIM)

    return jnp.concatenate([bd(wa), bd(wx)], axis=-1)


def kernel(x, norm1_g, w_in, lru_conv_w, lru_conv_b, lru_wa, lru_ba, lru_wx, lru_bx, lru_lambda,
           sc_conv_w, w_out, norm2_g, w_up, ffn_conv_w, w_down, final_g):
    bsz, seq, d_model = x.shape
    depth = w_in.shape[0]
    assert bsz == V7X_SUBLANES and (seq * bsz) % MIXER_ROWS == 0 and (seq * bsz) % FFN_ROWS == 0
    rows_of = lambda v: v.reshape(v.shape[0], 1, v.shape[-1])
    mixer_params = (rows_of(norm1_g), _pack_rows(w_in), lru_conv_w, rows_of(lru_conv_b),
                    _pack_rows(_block_diag_gates(lru_wa, lru_wx)), rows_of(lru_ba), rows_of(lru_bx),
                    rows_of(lru_lambda), sc_conv_w, _pack_rows(w_out))
    ffn_params = (rows_of(norm2_g), _pack_rows(w_up), ffn_conv_w, _pack_rows(w_down),
                  final_g.reshape(1, 1, d_model))
    xt = jnp.transpose(x, (1, 0, 2)).reshape(seq * bsz, d_model)
    for l in range(depth):
        xt = _mixer_call(xt, l, *mixer_params, batch=bsz)
        xt = _ffn_call(xt, l, *ffn_params, batch=bsz, final_norm=(l == depth - 1))
    return jnp.transpose(xt.reshape(seq, bsz, d_model), (1, 0, 2))
```

```python
import functools
import math

import jax
import jax.numpy as jnp
from jax import lax
from jax.experimental import pallas as pl
from jax.experimental.pallas import tpu as pltpu

EPS = 1e-6
RG_C = 8.0
LRU_CONV_WIDTH = 4
SC_CONV_WIDTH = 3
FFN_CONV_WIDTH = 3

V7X_SUBLANES = 8
V7X_MXU_DIM = 256
V7X_VMEM_BYTES = 64 * 1024 * 1024
VMEM_LIMIT_BYTES = V7X_VMEM_BYTES - 8 * 1024 * 1024

MIXER_ROWS = 256
FFN_ROWS = 512
FFN_CHUNK = 1024
BF16_PER_WORD = 2
PACK_ROWS = 256


def _rms_norm(x, g):
    ms = jnp.mean(x * x, axis=-1, keepdims=True)
    return x * lax.rsqrt(ms + EPS) * g


def _gelu_tanh(x):
    c = math.sqrt(2.0 / math.pi)
    return 0.5 * x * (1.0 + jnp.tanh(c * (x + 0.044715 * (x * x * x))))


def _sigmoid(x):
    return 0.5 * (jnp.tanh(0.5 * x) + 1.0)


def _softplus(x):
    return jnp.maximum(x, 0.0) + jnp.log1p(jnp.exp(-jnp.abs(x)))


def _weight(w_ref, r0, nrows, c0, ncols):
    p = BF16_PER_WORD
    return pltpu.bitcast(w_ref[r0 // p:(r0 + nrows) // p, c0:c0 + ncols], jnp.bfloat16)


def _causal_conv(ext_ref, w_ref, col0, ncols, rows, batch, width):
    acc = None
    for k in range(width):
        term = w_ref[k:k + 1, col0:col0 + ncols] * ext_ref[k * batch:k * batch + rows, col0:col0 + ncols]
        acc = term if acc is None else acc + term
    return acc


def _time_major(x):
    if x.ndim == 2:
        return x
    b, t, d = x.shape
    return pltpu.einshape("btd->tbd", x).reshape(t * b, d)


def _batch_major(y, batch):
    rows, d = y.shape
    return pltpu.einshape("tbd->btd", y.reshape(rows // batch, batch, d))


def _mixer_kernel(x_ref, g_ref, w_in_ref, cw_ref, cb_ref, wg_ref, ba_ref, bx_ref, lam_ref,
                  scw_ref, w_out_ref, o_ref,
                  zx_ext, v_ext, a_s, b_s, h_carry, ymix, *, batch, d_lru, d_sc):
    rows, d_model = o_ref.shape
    lru_halo = (LRU_CONV_WIDTH - 1) * batch
    sc_halo = (SC_CONV_WIDTH - 1) * batch

    @pl.when(pl.program_id(0) == 0)
    def _():
        zx_ext[0:lru_halo, :] = jnp.zeros((lru_halo, d_lru), jnp.float32)
        v_ext[0:sc_halo, :] = jnp.zeros((sc_halo, d_sc), jnp.float32)
        h_carry[...] = jnp.zeros_like(h_carry)

    x = _time_major(x_ref[...])
    hb = _rms_norm(x, g_ref[...]).astype(jnp.bfloat16)

    def proj(c0, n):
        return jnp.dot(hb, _weight(w_in_ref, 0, d_model, c0, n), preferred_element_type=jnp.float32)

    zx_ext[lru_halo:, :] = proj(0, d_lru)
    lru_x = _causal_conv(zx_ext, cw_ref, 0, d_lru, rows, batch, LRU_CONV_WIDTH) + cb_ref[...]
    zx_ext[0:lru_halo, :] = zx_ext[rows:rows + lru_halo, :]

    xb = lru_x.astype(jnp.bfloat16)
    gdim = V7X_MXU_DIM
    r_parts, i_parts = [], []
    for g in range(d_lru // gdim):
        rg = jnp.dot(xb[:, g * gdim:(g + 1) * gdim], _weight(wg_ref, g * gdim, gdim, 0, 2 * gdim),
                     preferred_element_type=jnp.float32)
        r_parts.append(rg[:, :gdim])
        i_parts.append(rg[:, gdim:])
    r = _sigmoid(jnp.concatenate(r_parts, axis=1) + ba_ref[...])
    ig = _sigmoid(jnp.concatenate(i_parts, axis=1) + bx_ref[...])
    log_a = r * (-RG_C * _softplus(-lam_ref[...]))
    a = jnp.exp(log_a)
    mult = jnp.sqrt(-jnp.tanh(log_a) * (a * a + 1.0))
    a_s[...] = a
    b_s[...] = mult * (ig * lru_x)

    h = h_carry[...]
    for t in range(rows // batch):
        r0 = t * batch
        h = a_s[r0:r0 + batch, :] * h + b_s[r0:r0 + batch, :]
        b_s[r0:r0 + batch, :] = h
    h_carry[...] = h
    y_lru = b_s[...] * _gelu_tanh(proj(d_lru, d_lru))
    ymix[:, 0:d_lru] = y_lru.astype(jnp.bfloat16)

    sc_b = proj(2 * d_lru, d_sc)
    sc_c = proj(2 * d_lru + d_sc, d_sc)
    sc_x = proj(2 * d_lru + 2 * d_sc, d_sc)
    v_ext[sc_halo:, :] = sc_c * sc_x
    y_sc = sc_b * _causal_conv(v_ext, scw_ref, 0, d_sc, rows, batch, SC_CONV_WIDTH)
    v_ext[0:sc_halo, :] = v_ext[rows:rows + sc_halo, :]
    ymix[:, d_lru:] = y_sc.astype(jnp.bfloat16)

    w_out = _weight(w_out_ref, 0, d_lru + d_sc, 0, d_model)
    o_ref[...] = x + jnp.dot(ymix[...], w_out, preferred_element_type=jnp.float32)


def _ffn_kernel(x_ref, g_ref, w_up_ref, cw_ref, w_down_ref, fg_ref, o_ref, u_ext,
                *, batch, d_ff, final_norm):
    rows, d_model = x_ref.shape
    halo = (FFN_CONV_WIDTH - 1) * batch
    n_chunks = d_ff // FFN_CHUNK

    @pl.when(pl.program_id(0) == 0)
    def _():
        u_ext[0:halo, :] = jnp.zeros((halo, 2 * d_ff), jnp.float32)

    x = x_ref[...]
    hb = _rms_norm(x, g_ref[...]).astype(jnp.bfloat16)

    def chunk_cols(c):
        return (c * FFN_CHUNK, d_ff + c * FFN_CHUNK)

    def up_project(c):
        for c0 in chunk_cols(c):
            u_ext[halo:, c0:c0 + FFN_CHUNK] = jnp.dot(
                hb, _weight(w_up_ref, 0, d_model, c0, FFN_CHUNK), preferred_element_type=jnp.float32)

    def conv(c0):
        y = _causal_conv(u_ext, cw_ref, c0, FFN_CHUNK, rows, batch, FFN_CONV_WIDTH)
        u_ext[0:halo, c0:c0 + FFN_CHUNK] = u_ext[rows:rows + halo, c0:c0 + FFN_CHUNK]
        return y

    acc = x
    up_project(0)
    for c in range(n_chunks):
        if c + 1 < n_chunks:
            up_project(c + 1)
        gate_c0, up_c0 = chunk_cols(c)
        act = (_gelu_tanh(conv(gate_c0)) * conv(up_c0)).astype(jnp.bfloat16)
        acc = acc + jnp.dot(act, _weight(w_down_ref, c * FFN_CHUNK, FFN_CHUNK, 0, d_model),
                            preferred_element_type=jnp.float32)
    if final_norm:
        acc = _rms_norm(acc, fg_ref[...])
    o_ref[...] = acc if o_ref.ndim == 2 else _batch_major(acc, batch)


def _resident(stacked, layer):
    _, a, b = stacked.shape
    return pl.BlockSpec((None, a, b), lambda i: (layer, 0, 0), pipeline_mode=pl.Buffered(1))


def _compiler_params():
    return pltpu.CompilerParams(dimension_semantics=("arbitrary",),
                                vmem_limit_bytes=VMEM_LIMIT_BYTES)


def _row_spec(rows, d_model):
    return pl.BlockSpec((rows, d_model), lambda i: (i, 0))


def _batch_major_spec(rows, batch, d_model):
    return pl.BlockSpec((batch, rows // batch, d_model), lambda i: (0, i, 0))


def _mixer_call(x, layer, g, w_in, cw, cb, wg, ba, bx, lam, scw, w_out, *, batch):
    d_model = x.shape[-1]
    n_rows = x.size // d_model
    d_lru = cw.shape[-1]
    d_sc = scw.shape[-1]
    rows = MIXER_ROWS
    row_spec = _row_spec(rows, d_model)
    operands = (g, w_in, cw, cb, wg, ba, bx, lam, scw, w_out)
    return pl.pallas_call(
        functools.partial(_mixer_kernel, batch=batch, d_lru=d_lru, d_sc=d_sc),
        grid=(n_rows // rows,),
        in_specs=[row_spec if x.ndim == 2 else _batch_major_spec(rows, batch, d_model)]
        + [_resident(a, layer) for a in operands],
        out_specs=row_spec,
        out_shape=jax.ShapeDtypeStruct((n_rows, d_model), x.dtype),
        scratch_shapes=[
            pltpu.VMEM((rows + (LRU_CONV_WIDTH - 1) * batch, d_lru), jnp.float32),
            pltpu.VMEM((rows + (SC_CONV_WIDTH - 1) * batch, d_sc), jnp.float32),
            pltpu.VMEM((rows, d_lru), jnp.float32),
            pltpu.VMEM((rows, d_lru), jnp.float32),
            pltpu.VMEM((batch, d_lru), jnp.float32),
            pltpu.VMEM((rows, d_lru + d_sc), jnp.bfloat16),
        ],
        compiler_params=_compiler_params(),
        name="mixer",
    )(x, *operands)


def _ffn_call(xt, layer, g, w_up, cw, w_down, fg, *, batch, final_norm, batch_major_out):
    n_rows, d_model = xt.shape
    d_ff = cw.shape[-1] // 2
    rows = FFN_ROWS
    row_spec = _row_spec(rows, d_model)
    operands = (g, w_up, cw, w_down)
    out_shape = (batch, n_rows // batch, d_model) if batch_major_out else (n_rows, d_model)
    return pl.pallas_call(
        functools.partial(_ffn_kernel, batch=batch, d_ff=d_ff, final_norm=final_norm),
        grid=(n_rows // rows,),
        in_specs=[row_spec] + [_resident(a, layer) for a in operands] + [_resident(fg, 0)],
        out_specs=_batch_major_spec(rows, batch, d_model) if batch_major_out else row_spec,
        out_shape=jax.ShapeDtypeStruct(out_shape, xt.dtype),
        scratch_shapes=[pltpu.VMEM((rows + (FFN_CONV_WIDTH - 1) * batch, 2 * d_ff), jnp.float32)],
        compiler_params=_compiler_params(),
        name="convffn",
    )(xt, *operands, fg)


def _pack_kernel(w_ref, o_ref):
    o_ref[...] = pltpu.bitcast(w_ref[...].astype(jnp.bfloat16), jnp.uint32)


def _pack_rows(w):
    layers, k, n = w.shape
    bk = min(k, PACK_ROWS)
    return pl.pallas_call(
        _pack_kernel,
        grid=(layers, k // bk),
        in_specs=[pl.BlockSpec((None, bk, n), lambda l, i: (l, i, 0))],
        out_specs=pl.BlockSpec((None, bk // BF16_PER_WORD, n), lambda l, i: (l, i, 0)),
        out_shape=jax.ShapeDtypeStruct((layers, k // BF16_PER_WORD, n), jnp.uint32),
        compiler_params=pltpu.CompilerParams(dimension_semantics=("arbitrary", "arbitrary"),
                                             vmem_limit_bytes=VMEM_LIMIT_BYTES),
        name="pack_weights",
    )(w)


def _block_diag_gates(wa, wx):
    layers, heads, hd, _ = wa.shape
    per = V7X_MXU_DIM // hd
    eye = jnp.eye(per, dtype=wa.dtype)

    def bd(w):
        w = w.reshape(layers, heads // per, per, hd, hd)
        return jnp.einsum('lgjio,jk->lgjiko', w, eye).reshape(layers, heads * hd, V7X_MXU_DIM)

    return jnp.concatenate([bd(wa), bd(wx)], axis=-1)


def kernel(x, norm1_g, w_in, lru_conv_w, lru_conv_b, lru_wa, lru_ba, lru_wx, lru_bx, lru_lambda,
           sc_conv_w, w_out, norm2_g, w_up, ffn_conv_w, w_down, final_g):
    bsz, seq, d_model = x.shape
    depth = w_in.shape[0]
    assert bsz == V7X_SUBLANES and (seq * bsz) % MIXER_ROWS == 0 and (seq * bsz) % FFN_ROWS == 0
    rows_of = lambda v: v.reshape(v.shape[0], 1, v.shape[-1])
    mixer_params = (rows_of(norm1_g), _pack_rows(w_in), lru_conv_w, rows_of(lru_conv_b),
                    _pack_rows(_block_diag_gates(lru_wa, lru_wx)), rows_of(lru_ba), rows_of(lru_bx),
                    rows_of(lru_lambda), sc_conv_w, _pack_rows(w_out))
    ffn_params = (rows_of(norm2_g), _pack_rows(w_up), ffn_conv_w, _pack_rows(w_down),
                  final_g.reshape(1, 1, d_model))
    for l in range(depth):
        last = l == depth - 1
        x = _mixer_call(x, l, *mixer_params, batch=bsz)
        x = _ffn_call(x, l, *ffn_params, batch=bsz, final_norm=last, batch_major_out=last)
    return x
```

```python
import functools
import math

import jax
import jax.numpy as jnp
from jax import lax
from jax.experimental import pallas as pl
from jax.experimental.pallas import tpu as pltpu

EPS = 1e-6
RG_C = 8.0
LRU_CONV_WIDTH = 4
SC_CONV_WIDTH = 3
FFN_CONV_WIDTH = 3

V7X_SUBLANES = 8
V7X_MXU_DIM = 256
V7X_VMEM_BYTES = 64 * 1024 * 1024
VMEM_LIMIT_BYTES = V7X_VMEM_BYTES - 8 * 1024 * 1024

MIXER_ROWS = 512
FFN_ROWS = 512
FFN_CHUNK = 1024
BF16_PER_WORD = 2
PACK_ROWS = 256


def _rms_norm(x, g):
    ms = jnp.mean(x * x, axis=-1, keepdims=True)
    return x * lax.rsqrt(ms + EPS) * g


def _gelu_tanh(x):
    c = math.sqrt(2.0 / math.pi)
    half_x = 0.5 * x
    return half_x + half_x * jnp.tanh(x * (c + (c * 0.044715) * (x * x)))


def _sigmoid(x):
    return 0.5 * (jnp.tanh(0.5 * x) + 1.0)


def _softplus(x):
    return jnp.maximum(x, 0.0) + jnp.log1p(jnp.exp(-jnp.abs(x)))


def _weight(w_ref, r0, nrows, c0, ncols):
    p = BF16_PER_WORD
    return pltpu.bitcast(w_ref[r0 // p:(r0 + nrows) // p, c0:c0 + ncols], jnp.bfloat16)


def _causal_conv(ext_ref, w_ref, col0, ncols, rows, batch, width):
    acc = None
    for k in range(width):
        term = w_ref[k:k + 1, col0:col0 + ncols] * ext_ref[k * batch:k * batch + rows, col0:col0 + ncols]
        acc = term if acc is None else acc + term
    return acc


def _time_major(x):
    if x.ndim == 2:
        return x
    b, t, d = x.shape
    return jnp.swapaxes(x, 0, 1).reshape(t * b, d)


def _batch_major(y, batch):
    rows, d = y.shape
    return jnp.swapaxes(y.reshape(rows // batch, batch, d), 0, 1)


def _mixer_kernel(x_ref, g_ref, w_in_ref, cw_ref, cb_ref, wg_ref, ba_ref, bx_ref, lam_ref,
                  scw_ref, w_out_ref, o_ref,
                  zx_ext, v_ext, a_s, b_s, h_carry, ymix, *, batch, d_lru, d_sc):
    rows, d_model = o_ref.shape
    lru_halo = (LRU_CONV_WIDTH - 1) * batch
    sc_halo = (SC_CONV_WIDTH - 1) * batch

    @pl.when(pl.program_id(0) == 0)
    def _():
        zx_ext[0:lru_halo, :] = jnp.zeros((lru_halo, d_lru), jnp.float32)
        v_ext[0:sc_halo, :] = jnp.zeros((sc_halo, d_sc), jnp.float32)
        h_carry[...] = jnp.zeros_like(h_carry)

    x = _time_major(x_ref[...])
    hb = _rms_norm(x, g_ref[...]).astype(jnp.bfloat16)

    def proj(c0, n):
        return jnp.dot(hb, _weight(w_in_ref, 0, d_model, c0, n), preferred_element_type=jnp.float32)

    zx_ext[lru_halo:, :] = proj(0, d_lru)
    lru_x = _causal_conv(zx_ext, cw_ref, 0, d_lru, rows, batch, LRU_CONV_WIDTH) + cb_ref[...]
    zx_ext[0:lru_halo, :] = zx_ext[rows:rows + lru_halo, :]

    xb = lru_x.astype(jnp.bfloat16)
    gdim = V7X_MXU_DIM
    r_parts, i_parts = [], []
    for g in range(d_lru // gdim):
        rg = jnp.dot(xb[:, g * gdim:(g + 1) * gdim], _weight(wg_ref, g * gdim, gdim, 0, 2 * gdim),
                     preferred_element_type=jnp.float32)
        r_parts.append(rg[:, :gdim])
        i_parts.append(rg[:, gdim:])
    r = _sigmoid(jnp.concatenate(r_parts, axis=1) + ba_ref[...])
    ig = _sigmoid(jnp.concatenate(i_parts, axis=1) + bx_ref[...])
    log_a = r * (-RG_C * _softplus(-lam_ref[...]))
    a = jnp.exp(log_a)
    gain_sq = -jnp.tanh(log_a) * (a * a + 1.0)
    mult = jnp.where(gain_sq > 0.0, gain_sq * lax.rsqrt(gain_sq), 0.0)
    a_s[...] = a
    b_s[...] = mult * (ig * lru_x)

    h = h_carry[...]
    for t in range(rows // batch):
        r0 = t * batch
        h = a_s[r0:r0 + batch, :] * h + b_s[r0:r0 + batch, :]
        b_s[r0:r0 + batch, :] = h
    h_carry[...] = h
    y_lru = b_s[...] * _gelu_tanh(proj(d_lru, d_lru))
    ymix[:, 0:d_lru] = y_lru.astype(jnp.bfloat16)

    sc_b = proj(2 * d_lru, d_sc)
    sc_c = proj(2 * d_lru + d_sc, d_sc)
    sc_x = proj(2 * d_lru + 2 * d_sc, d_sc)
    v_ext[sc_halo:, :] = sc_c * sc_x
    y_sc = sc_b * _causal_conv(v_ext, scw_ref, 0, d_sc, rows, batch, SC_CONV_WIDTH)
    v_ext[0:sc_halo, :] = v_ext[rows:rows + sc_halo, :]
    ymix[:, d_lru:] = y_sc.astype(jnp.bfloat16)

    w_out = _weight(w_out_ref, 0, d_lru + d_sc, 0, d_model)
    o_ref[...] = x + jnp.dot(ymix[...], w_out, preferred_element_type=jnp.float32)


def _ffn_kernel(x_ref, g_ref, w_up_ref, cw_ref, w_down_ref, fg_ref, o_ref, u_ext,
                *, batch, d_ff, final_norm):
    rows, d_model = x_ref.shape
    halo = (FFN_CONV_WIDTH - 1) * batch
    n_chunks = d_ff // FFN_CHUNK

    @pl.when(pl.program_id(0) == 0)
    def _():
        u_ext[0:halo, :] = jnp.zeros((halo, 2 * d_ff), jnp.float32)

    x = x_ref[...]
    hb = _rms_norm(x, g_ref[...]).astype(jnp.bfloat16)

    def chunk_cols(c):
        return (c * FFN_CHUNK, d_ff + c * FFN_CHUNK)

    def up_project(c):
        for c0 in chunk_cols(c):
            u_ext[halo:, c0:c0 + FFN_CHUNK] = jnp.dot(
                hb, _weight(w_up_ref, 0, d_model, c0, FFN_CHUNK), preferred_element_type=jnp.float32)

    def conv(c0):
        y = _causal_conv(u_ext, cw_ref, c0, FFN_CHUNK, rows, batch, FFN_CONV_WIDTH)
        u_ext[0:halo, c0:c0 + FFN_CHUNK] = u_ext[rows:rows + halo, c0:c0 + FFN_CHUNK]
        return y

    acc = x
    up_project(0)
    for c in range(n_chunks):
        if c + 1 < n_chunks:
            up_project(c + 1)
        gate_c0, up_c0 = chunk_cols(c)
        act = (_gelu_tanh(conv(gate_c0)) * conv(up_c0)).astype(jnp.bfloat16)
        acc = acc + jnp.dot(act, _weight(w_down_ref, c * FFN_CHUNK, FFN_CHUNK, 0, d_model),
                            preferred_element_type=jnp.float32)
    if final_norm:
        acc = _rms_norm(acc, fg_ref[...])
    o_ref[...] = acc if o_ref.ndim == 2 else _batch_major(acc, batch)


def _resident(stacked, layer):
    _, a, b = stacked.shape
    return pl.BlockSpec((None, a, b), lambda i: (layer, 0, 0), pipeline_mode=pl.Buffered(1))


def _compiler_params():
    return pltpu.CompilerParams(dimension_semantics=("arbitrary",),
                                vmem_limit_bytes=VMEM_LIMIT_BYTES)


def _row_spec(rows, d_model):
    return pl.BlockSpec((rows, d_model), lambda i: (i, 0))


def _batch_major_spec(rows, batch, d_model):
    return pl.BlockSpec((batch, rows // batch, d_model), lambda i: (0, i, 0))


def _mixer_call(x, layer, g, w_in, cw, cb, wg, ba, bx, lam, scw, w_out, *, batch):
    d_model = x.shape[-1]
    n_rows = x.size // d_model
    d_lru = cw.shape[-1]
    d_sc = scw.shape[-1]
    rows = MIXER_ROWS
    row_spec = _row_spec(rows, d_model)
    operands = (g, w_in, cw, cb, wg, ba, bx, lam, scw, w_out)
    return pl.pallas_call(
        functools.partial(_mixer_kernel, batch=batch, d_lru=d_lru, d_sc=d_sc),
        grid=(n_rows // rows,),
        in_specs=[row_spec if x.ndim == 2 else _batch_major_spec(rows, batch, d_model)]
        + [_resident(a, layer) for a in operands],
        out_specs=row_spec,
        out_shape=jax.ShapeDtypeStruct((n_rows, d_model), x.dtype),
        scratch_shapes=[
            pltpu.VMEM((rows + (LRU_CONV_WIDTH - 1) * batch, d_lru), jnp.float32),
            pltpu.VMEM((rows + (SC_CONV_WIDTH - 1) * batch, d_sc), jnp.float32),
            pltpu.VMEM((rows, d_lru), jnp.float32),
            pltpu.VMEM((rows, d_lru), jnp.float32),
            pltpu.VMEM((batch, d_lru), jnp.float32),
            pltpu.VMEM((rows, d_lru + d_sc), jnp.bfloat16),
        ],
        compiler_params=_compiler_params(),
        name="mixer",
    )(x, *operands)


def _ffn_call(xt, layer, g, w_up, cw, w_down, fg, *, batch, final_norm, batch_major_out):
    n_rows, d_model = xt.shape
    d_ff = cw.shape[-1] // 2
    rows = FFN_ROWS
    row_spec = _row_spec(rows, d_model)
    operands = (g, w_up, cw, w_down)
    out_shape = (batch, n_rows // batch, d_model) if batch_major_out else (n_rows, d_model)
    return pl.pallas_call(
        functools.partial(_ffn_kernel, batch=batch, d_ff=d_ff, final_norm=final_norm),
        grid=(n_rows // rows,),
        in_specs=[row_spec] + [_resident(a, layer) for a in operands] + [_resident(fg, 0)],
        out_specs=_batch_major_spec(rows, batch, d_model) if batch_major_out else row_spec,
        out_shape=jax.ShapeDtypeStruct(out_shape, xt.dtype),
        scratch_shapes=[pltpu.VMEM((rows + (FFN_CONV_WIDTH - 1) * batch, 2 * d_ff), jnp.float32)],
        compiler_params=_compiler_params(),
        name="convffn",
    )(xt, *operands, fg)


def _pack_kernel(w_ref, o_ref):
    o_ref[...] = pltpu.bitcast(w_ref[...].astype(jnp.bfloat16), jnp.uint32)


def _pack_rows(w):
    layers, k, n = w.shape
    bk = min(k, PACK_ROWS)
    return pl.pallas_call(
        _pack_kernel,
        grid=(layers, k // bk),
        in_specs=[pl.BlockSpec((None, bk, n), lambda l, i: (l, i, 0))],
        out_specs=pl.BlockSpec((None, bk // BF16_PER_WORD, n), lambda l, i: (l, i, 0)),
        out_shape=jax.ShapeDtypeStruct((layers, k // BF16_PER_WORD, n), jnp.uint32),
        compiler_params=pltpu.CompilerParams(dimension_semantics=("arbitrary", "arbitrary"),
                                             vmem_limit_bytes=VMEM_LIMIT_BYTES),
        name="pack_weights",
    )(w)


def _block_diag_gates(wa, wx):
    layers, heads, hd, _ = wa.shape
    per = V7X_MXU_DIM // hd
    eye = jnp.eye(per, dtype=wa.dtype)

    def bd(w):
        w = w.reshape(layers, heads // per, per, hd, hd)
        return jnp.einsum('lgjio,jk->lgjiko', w, eye).reshape(layers, heads * hd, V7X_MXU_DIM)

    return jnp.concatenate([bd(wa), bd(wx)], axis=-1)


def kernel(x, norm1_g, w_in, lru_conv_w, lru_conv_b, lru_wa, lru_ba, lru_wx, lru_bx, lru_lambda,
           sc_conv_w, w_out, norm2_g, w_up, ffn_conv_w, w_down, final_g):
    bsz, seq, d_model = x.shape
    depth = w_in.shape[0]
    assert bsz == V7X_SUBLANES and (seq * bsz) % MIXER_ROWS == 0 and (seq * bsz) % FFN_ROWS == 0
    rows_of = lambda v: v.reshape(v.shape[0], 1, v.shape[-1])
    mixer_params = (rows_of(norm1_g), _pack_rows(w_in), lru_conv_w, rows_of(lru_conv_b),
                    _pack_rows(_block_diag_gates(lru_wa, lru_wx)), rows_of(lru_ba), rows_of(lru_bx),
                    rows_of(lru_lambda), sc_conv_w, _pack_rows(w_out))
    ffn_params = (rows_of(norm2_g), _pack_rows(w_up), ffn_conv_w, _pack_rows(w_down),
                  final_g.reshape(1, 1, d_model))
    for l in range(depth):
        last = l == depth - 1
        x = _mixer_call(x, l, *mixer_params, batch=bsz)
        x = _ffn_call(x, l, *ffn_params, batch=bsz, final_norm=last, batch_major_out=last)
    return x
```

```python
import functools
import math

import jax
import jax.numpy as jnp
from jax import lax
from jax.experimental import pallas as pl
from jax.experimental.pallas import tpu as pltpu

EPS = 1e-6
RG_C = 8.0
LRU_CONV_WIDTH = 4
SC_CONV_WIDTH = 3
FFN_CONV_WIDTH = 3

V7X_SUBLANES = 8
V7X_MXU_DIM = 256
V7X_VMEM_BYTES = 64 * 1024 * 1024
VMEM_LIMIT_BYTES = V7X_VMEM_BYTES - 8 * 1024 * 1024

MIXER_ROWS = 512
MIXER_WEIGHT_STEPS = 4
FFN_ROWS = 512
FFN_CHUNK = 1024
FFN_WEIGHT_STEPS = 8
BF16_PER_WORD = 2


def _rms_norm(x, g):
    ms = jnp.mean(x * x, axis=-1, keepdims=True)
    return x * lax.rsqrt(ms + EPS) * g


def _gelu_tanh(x):
    c = math.sqrt(2.0 / math.pi)
    half_x = 0.5 * x
    return half_x + half_x * jnp.tanh(x * (c + (c * 0.044715) * (x * x)))


def _sigmoid(x):
    return 0.5 * (jnp.tanh(0.5 * x) + 1.0)


def _softplus(x):
    return jnp.maximum(x, 0.0) + jnp.log1p(jnp.exp(-jnp.abs(x)))


def _store_weight_chunk(step, chunk_ref, w_scr):
    half = chunk_ref.shape[0] // BF16_PER_WORD
    r0 = pl.multiple_of(step * half, half)
    w_scr[pl.ds(r0, half), :] = pltpu.bitcast(chunk_ref[...].astype(jnp.bfloat16), jnp.uint32)


def _weight(w_scr, r0, nrows, c0, ncols):
    p = BF16_PER_WORD
    return pltpu.bitcast(w_scr[r0 // p:(r0 + nrows) // p, c0:c0 + ncols], jnp.bfloat16)


def _causal_conv(ext_ref, w_ref, col0, ncols, rows, batch, width):
    acc = None
    for k in range(width):
        term = w_ref[k:k + 1, col0:col0 + ncols] * ext_ref[k * batch:k * batch + rows, col0:col0 + ncols]
        acc = term if acc is None else acc + term
    return acc


def _time_major(x):
    if x.ndim == 2:
        return x
    b, t, d = x.shape
    return jnp.swapaxes(x, 0, 1).reshape(t * b, d)


def _batch_major(y, batch):
    rows, d = y.shape
    return jnp.swapaxes(y.reshape(rows // batch, batch, d), 0, 1)


def _mixer_rows(x_ref, g_ref, cw_ref, cb_ref, ba_ref, bx_ref, lam_ref, scw_ref, o_ref,
                w_in_s, wg_s, w_out_s, zx_ext, v_ext, a_s, b_s, h_carry, ymix, *, batch, d_lru, d_sc):
    rows, d_model = o_ref.shape
    lru_halo = (LRU_CONV_WIDTH - 1) * batch
    sc_halo = (SC_CONV_WIDTH - 1) * batch

    x = _time_major(x_ref[...])
    hb = _rms_norm(x, g_ref[...]).astype(jnp.bfloat16)

    def proj(c0, n):
        return jnp.dot(hb, _weight(w_in_s, 0, d_model, c0, n), preferred_element_type=jnp.float32)

    zx_ext[lru_halo:, :] = proj(0, d_lru)
    lru_x = _causal_conv(zx_ext, cw_ref, 0, d_lru, rows, batch, LRU_CONV_WIDTH) + cb_ref[...]
    zx_ext[0:lru_halo, :] = zx_ext[rows:rows + lru_halo, :]

    xb = lru_x.astype(jnp.bfloat16)
    gdim = V7X_MXU_DIM
    r_parts, i_parts = [], []
    for g in range(d_lru // gdim):
        rg = jnp.dot(xb[:, g * gdim:(g + 1) * gdim], _weight(wg_s, g * gdim, gdim, 0, 2 * gdim),
                     preferred_element_type=jnp.float32)
        r_parts.append(rg[:, :gdim])
        i_parts.append(rg[:, gdim:])
    r = _sigmoid(jnp.concatenate(r_parts, axis=1) + ba_ref[...])
    ig = _sigmoid(jnp.concatenate(i_parts, axis=1) + bx_ref[...])
    log_a = r * (-RG_C * _softplus(-lam_ref[...]))
    a = jnp.exp(log_a)
    gain_sq = -jnp.tanh(log_a) * (a * a + 1.0)
    mult = jnp.where(gain_sq > 0.0, gain_sq * lax.rsqrt(gain_sq), 0.0)
    a_s[...] = a
    b_s[...] = mult * (ig * lru_x)

    h = h_carry[...]
    for t in range(rows // batch):
        r0 = t * batch
        h = a_s[r0:r0 + batch, :] * h + b_s[r0:r0 + batch, :]
        b_s[r0:r0 + batch, :] = h
    h_carry[...] = h
    y_lru = b_s[...] * _gelu_tanh(proj(d_lru, d_lru))
    ymix[:, 0:d_lru] = y_lru.astype(jnp.bfloat16)

    sc_b = proj(2 * d_lru, d_sc)
    sc_c = proj(2 * d_lru + d_sc, d_sc)
    sc_x = proj(2 * d_lru + 2 * d_sc, d_sc)
    v_ext[sc_halo:, :] = sc_c * sc_x
    y_sc = sc_b * _causal_conv(v_ext, scw_ref, 0, d_sc, rows, batch, SC_CONV_WIDTH)
    v_ext[0:sc_halo, :] = v_ext[rows:rows + sc_halo, :]
    ymix[:, d_lru:] = y_sc.astype(jnp.bfloat16)

    w_out = _weight(w_out_s, 0, d_lru + d_sc, 0, d_model)
    o_ref[...] = x + jnp.dot(ymix[...], w_out, preferred_element_type=jnp.float32)


def _mixer_kernel(x_ref, g_ref, w_in_chunk, cw_ref, cb_ref, wg_chunk, ba_ref, bx_ref, lam_ref,
                  scw_ref, w_out_chunk, o_ref,
                  w_in_s, wg_s, w_out_s, zx_ext, v_ext, a_s, b_s, h_carry, ymix, *, batch, d_lru, d_sc):
    step = pl.program_id(0)

    @pl.when(step == 0)
    def _():
        zx_ext[0:(LRU_CONV_WIDTH - 1) * batch, :] = jnp.zeros(((LRU_CONV_WIDTH - 1) * batch, d_lru), jnp.float32)
        v_ext[0:(SC_CONV_WIDTH - 1) * batch, :] = jnp.zeros(((SC_CONV_WIDTH - 1) * batch, d_sc), jnp.float32)
        h_carry[...] = jnp.zeros_like(h_carry)

    @pl.when(step < MIXER_WEIGHT_STEPS)
    def _():
        _store_weight_chunk(step, w_in_chunk, w_in_s)
        _store_weight_chunk(step, wg_chunk, wg_s)
        _store_weight_chunk(step, w_out_chunk, w_out_s)

    @pl.when(step >= MIXER_WEIGHT_STEPS)
    def _():
        _mixer_rows(x_ref, g_ref, cw_ref, cb_ref, ba_ref, bx_ref, lam_ref, scw_ref, o_ref,
                    w_in_s, wg_s, w_out_s, zx_ext, v_ext, a_s, b_s, h_carry, ymix,
                    batch=batch, d_lru=d_lru, d_sc=d_sc)


def _ffn_rows(x_ref, g_ref, cw_ref, fg_ref, o_ref, w_up_s, w_down_s, u_ext, *, batch, d_ff, final_norm):
    rows, d_model = x_ref.shape
    halo = (FFN_CONV_WIDTH - 1) * batch
    n_chunks = d_ff // FFN_CHUNK

    x = x_ref[...]
    hb = _rms_norm(x, g_ref[...]).astype(jnp.bfloat16)

    def chunk_cols(c):
        return (c * FFN_CHUNK, d_ff + c * FFN_CHUNK)

    def up_project(c):
        for c0 in chunk_cols(c):
            u_ext[halo:, c0:c0 + FFN_CHUNK] = jnp.dot(
                hb, _weight(w_up_s, 0, d_model, c0, FFN_CHUNK), preferred_element_type=jnp.float32)

    def conv(c0):
        y = _causal_conv(u_ext, cw_ref, c0, FFN_CHUNK, rows, batch, FFN_CONV_WIDTH)
        u_ext[0:halo, c0:c0 + FFN_CHUNK] = u_ext[rows:rows + halo, c0:c0 + FFN_CHUNK]
        return y

    acc = x
    up_project(0)
    for c in range(n_chunks):
        if c + 1 < n_chunks:
            up_project(c + 1)
        gate_c0, up_c0 = chunk_cols(c)
        act = (_gelu_tanh(conv(gate_c0)) * conv(up_c0)).astype(jnp.bfloat16)
        acc = acc + jnp.dot(act, _weight(w_down_s, c * FFN_CHUNK, FFN_CHUNK, 0, d_model),
                            preferred_element_type=jnp.float32)
    if final_norm:
        acc = _rms_norm(acc, fg_ref[...])
    o_ref[...] = acc if o_ref.ndim == 2 else _batch_major(acc, batch)


def _ffn_kernel(x_ref, g_ref, w_up_chunk, cw_ref, w_down_chunk, fg_ref, o_ref,
                w_up_s, w_down_s, u_ext, *, batch, d_ff, final_norm):
    step = pl.program_id(0)

    @pl.when(step == 0)
    def _():
        u_ext[0:(FFN_CONV_WIDTH - 1) * batch, :] = jnp.zeros(((FFN_CONV_WIDTH - 1) * batch, 2 * d_ff), jnp.float32)

    @pl.when(step < FFN_WEIGHT_STEPS)
    def _():
        _store_weight_chunk(step, w_up_chunk, w_up_s)
        _store_weight_chunk(step, w_down_chunk, w_down_s)

    @pl.when(step >= FFN_WEIGHT_STEPS)
    def _():
        _ffn_rows(x_ref, g_ref, cw_ref, fg_ref, o_ref, w_up_s, w_down_s, u_ext,
                  batch=batch, d_ff=d_ff, final_norm=final_norm)


def _resident(stacked, layer):
    _, a, b = stacked.shape
    return pl.BlockSpec((None, a, b), lambda i: (layer, 0, 0), pipeline_mode=pl.Buffered(1))


def _weight_chunks(stacked, layer, n_steps):
    _, k, n = stacked.shape
    return pl.BlockSpec((None, k // n_steps, n), lambda i: (layer, jnp.minimum(i, n_steps - 1), 0))


def _packed_weight_scratch(stacked):
    _, k, n = stacked.shape
    return pltpu.VMEM((k // BF16_PER_WORD, n), jnp.uint32)


def _row_spec(rows, d_model, weight_steps):
    return pl.BlockSpec((rows, d_model), lambda i: (jnp.maximum(i - weight_steps, 0), 0))


def _batch_major_spec(rows, batch, d_model, weight_steps):
    return pl.BlockSpec((batch, rows // batch, d_model), lambda i: (0, jnp.maximum(i - weight_steps, 0), 0))


def _compiler_params():
    return pltpu.CompilerParams(dimension_semantics=("arbitrary",),
                                vmem_limit_bytes=VMEM_LIMIT_BYTES)


def _mixer_call(x, layer, g, w_in, cw, cb, wg, ba, bx, lam, scw, w_out, *, batch):
    d_model = x.shape[-1]
    n_rows = x.size // d_model
    d_lru = cw.shape[-1]
    d_sc = scw.shape[-1]
    rows = MIXER_ROWS
    ws = MIXER_WEIGHT_STEPS
    row_spec = _row_spec(rows, d_model, ws)
    res = lambda a: _resident(a, layer)
    chunks = lambda a: _weight_chunks(a, layer, ws)
    return pl.pallas_call(
        functools.partial(_mixer_kernel, batch=batch, d_lru=d_lru, d_sc=d_sc),
        grid=(ws + n_rows // rows,),
        in_specs=[row_spec if x.ndim == 2 else _batch_major_spec(rows, batch, d_model, ws),
                  res(g), chunks(w_in), res(cw), res(cb), chunks(wg), res(ba), res(bx), res(lam),
                  res(scw), chunks(w_out)],
        out_specs=row_spec,
        out_shape=jax.ShapeDtypeStruct((n_rows, d_model), x.dtype),
        scratch_shapes=[
            _packed_weight_scratch(w_in), _packed_weight_scratch(wg), _packed_weight_scratch(w_out),
            pltpu.VMEM((rows + (LRU_CONV_WIDTH - 1) * batch, d_lru), jnp.float32),
            pltpu.VMEM((rows + (SC_CONV_WIDTH - 1) * batch, d_sc), jnp.float32),
            pltpu.VMEM((rows, d_lru), jnp.float32),
            pltpu.VMEM((rows, d_lru), jnp.float32),
            pltpu.VMEM((batch, d_lru), jnp.float32),
            pltpu.VMEM((rows, d_lru + d_sc), jnp.bfloat16),
        ],
        compiler_params=_compiler_params(),
        name="mixer",
    )(x, g, w_in, cw, cb, wg, ba, bx, lam, scw, w_out)


def _ffn_call(xt, layer, g, w_up, cw, w_down, fg, *, batch, final_norm, batch_major_out):
    n_rows, d_model = xt.shape
    d_ff = cw.shape[-1] // 2
    rows = FFN_ROWS
    ws = FFN_WEIGHT_STEPS
    row_spec = _row_spec(rows, d_model, ws)
    out_shape = (batch, n_rows // batch, d_model) if batch_major_out else (n_rows, d_model)
    return pl.pallas_call(
        functools.partial(_ffn_kernel, batch=batch, d_ff=d_ff, final_norm=final_norm),
        grid=(ws + n_rows // rows,),
        in_specs=[row_spec, _resident(g, layer), _weight_chunks(w_up, layer, ws), _resident(cw, layer),
                  _weight_chunks(w_down, layer, ws), _resident(fg, 0)],
        out_specs=_batch_major_spec(rows, batch, d_model, ws) if batch_major_out else row_spec,
        out_shape=jax.ShapeDtypeStruct(out_shape, xt.dtype),
        scratch_shapes=[_packed_weight_scratch(w_up), _packed_weight_scratch(w_down),
                        pltpu.VMEM((rows + (FFN_CONV_WIDTH - 1) * batch, 2 * d_ff), jnp.float32)],
        compiler_params=_compiler_params(),
        name="convffn",
    )(xt, g, w_up, cw, w_down, fg)


def _block_diag_gates(wa, wx):
    layers, heads, hd, _ = wa.shape
    per = V7X_MXU_DIM // hd
    eye = jnp.eye(per, dtype=wa.dtype)

    def bd(w):
        w = w.reshape(layers, heads // per, per, hd, hd)
        return jnp.einsum('lgjio,jk->lgjiko', w, eye).reshape(layers, heads * hd, V7X_MXU_DIM)

    return jnp.concatenate([bd(wa), bd(wx)], axis=-1)


def kernel(x, norm1_g, w_in, lru_conv_w, lru_conv_b, lru_wa, lru_ba, lru_wx, lru_bx, lru_lambda,
           sc_conv_w, w_out, norm2_g, w_up, ffn_conv_w, w_down, final_g):
    bsz, seq, d_model = x.shape
    depth = w_in.shape[0]
    assert bsz == V7X_SUBLANES and (seq * bsz) % MIXER_ROWS == 0 and (seq * bsz) % FFN_ROWS == 0
    rows_of = lambda v: v.reshape(v.shape[0], 1, v.shape[-1])
    mixer_params = (rows_of(norm1_g), w_in, lru_conv_w, rows_of(lru_conv_b),
                    _block_diag_gates(lru_wa, lru_wx), rows_of(lru_ba), rows_of(lru_bx),
                    rows_of(lru_lambda), sc_conv_w, w_out)
    ffn_params = (rows_of(norm2_g), w_up, ffn_conv_w, w_down, final_g.reshape(1, 1, d_model))
    for l in range(depth):
        last = l == depth - 1
        x = _mixer_call(x, l, *mixer_params, batch=bsz)
        x = _ffn_call(x, l, *ffn_params, batch=bsz, final_norm=last, batch_major_out=last)
    return x
```

```python
import functools
import math

import jax
import jax.numpy as jnp
from jax import lax
from jax.experimental import pallas as pl
from jax.experimental.pallas import tpu as pltpu

EPS = 1e-6
RG_C = 8.0
LRU_CONV_WIDTH = 4
SC_CONV_WIDTH = 3
FFN_CONV_WIDTH = 3

V7X_SUBLANES = 8
V7X_MXU_DIM = 256
V7X_VMEM_BYTES = 64 * 1024 * 1024
VMEM_LIMIT_BYTES = V7X_VMEM_BYTES - 8 * 1024 * 1024

MIXER_ROWS = 512
MIXER_WEIGHT_STEPS = 4
FFN_ROWS = 512
FFN_CHUNK = 1024
FFN_WEIGHT_STEPS = 8
BF16_PER_WORD = 2


def _rms_norm(x, g):
    ms = jnp.mean(x * x, axis=-1, keepdims=True)
    return x * lax.rsqrt(ms + EPS) * g


def _gelu_tanh(x):
    c = math.sqrt(2.0 / math.pi)
    half_x = 0.5 * x
    return half_x + half_x * jnp.tanh(x * (c + (c * 0.044715) * (x * x)))


def _sigmoid(x):
    return 0.5 * (jnp.tanh(0.5 * x) + 1.0)


def _softplus(x):
    return jnp.maximum(x, 0.0) + jnp.log1p(jnp.exp(-jnp.abs(x)))


def _store_weight_chunk(step, chunk_ref, w_scr):
    half = chunk_ref.shape[0] // BF16_PER_WORD
    r0 = pl.multiple_of(step * half, half)
    w_scr[pl.ds(r0, half), :] = pltpu.bitcast(chunk_ref[...].astype(jnp.bfloat16), jnp.uint32)


def _weight(w_scr, r0, nrows, c0, ncols):
    p = BF16_PER_WORD
    return pltpu.bitcast(w_scr[r0 // p:(r0 + nrows) // p, c0:c0 + ncols], jnp.bfloat16)


def _causal_conv(ext_ref, w_ref, col0, ncols, rows, batch, width):
    acc = None
    for k in range(width):
        term = w_ref[k:k + 1, col0:col0 + ncols] * ext_ref[k * batch:k * batch + rows, col0:col0 + ncols]
        acc = term if acc is None else acc + term
    return acc


def _time_major(x):
    if x.ndim == 2:
        return x
    b, t, d = x.shape
    return jnp.swapaxes(x, 0, 1).reshape(t * b, d)


def _batch_major(y, batch):
    rows, d = y.shape
    return jnp.swapaxes(y.reshape(rows // batch, batch, d), 0, 1)


def _mixer_rows(x_ref, g_ref, cw_ref, cb_ref, ba_ref, bx_ref, lam_ref, scw_ref, o_ref,
                w_in_s, wg_s, w_out_s, zx_ext, v_ext, a_s, b_s, h_carry, *, batch, d_lru, d_sc):
    rows, d_model = o_ref.shape
    lru_halo = (LRU_CONV_WIDTH - 1) * batch
    sc_halo = (SC_CONV_WIDTH - 1) * batch
    gdim = V7X_MXU_DIM
    n_groups = d_lru // gdim

    x = _time_major(x_ref[...])
    hb = _rms_norm(x, g_ref[...]).astype(jnp.bfloat16)
    c_lam = -RG_C * _softplus(-lam_ref[...])

    def proj(c0, n):
        return jnp.dot(hb, _weight(w_in_s, 0, d_model, c0, n), preferred_element_type=jnp.float32)

    out_gate, gate_pre, lru_in = {}, {}, {}

    def lru_project(g):
        z = proj(2 * g * gdim, 2 * gdim)
        zx_ext[lru_halo:, g * gdim:(g + 1) * gdim] = z[:, :gdim]
        out_gate[g] = z[:, gdim:]

    def lru_gates(g):
        cols = slice(g * gdim, (g + 1) * gdim)
        lru_x = _causal_conv(zx_ext, cw_ref, g * gdim, gdim, rows, batch, LRU_CONV_WIDTH) + cb_ref[:, cols]
        zx_ext[0:lru_halo, cols] = zx_ext[rows:rows + lru_halo, cols]
        lru_in[g] = lru_x
        gate_pre[g] = jnp.dot(lru_x.astype(jnp.bfloat16), _weight(wg_s, g * gdim, gdim, 0, 2 * gdim),
                              preferred_element_type=jnp.float32)

    def lru_recurrence(g):
        cols = slice(g * gdim, (g + 1) * gdim)
        rg, lru_x = gate_pre.pop(g), lru_in.pop(g)
        r = _sigmoid(rg[:, :gdim] + ba_ref[:, cols])
        ig = _sigmoid(rg[:, gdim:] + bx_ref[:, cols])
        log_a = r * c_lam[:, cols]
        a = jnp.exp(log_a)
        gain_sq = -jnp.tanh(log_a) * (a * a + 1.0)
        mult = jnp.where(gain_sq > 0.0, gain_sq * lax.rsqrt(gain_sq), 0.0)
        a_s[:, cols] = a
        b_s[:, cols] = mult * (ig * lru_x)
        h = h_carry[:, cols]
        for t in range(rows // batch):
            r0 = t * batch
            h = a_s[r0:r0 + batch, cols] * h + b_s[r0:r0 + batch, cols]
            b_s[r0:r0 + batch, cols] = h
        h_carry[:, cols] = h
        return (b_s[:, cols] * _gelu_tanh(out_gate.pop(g))).astype(jnp.bfloat16)

    def out_project(y, r0):
        return jnp.dot(y, _weight(w_out_s, r0, y.shape[1], 0, d_model), preferred_element_type=jnp.float32)

    sc_c0 = 2 * d_lru
    acc = x
    lru_project(0)
    lru_project(1)
    lru_gates(0)
    for g in range(n_groups):
        if g + 2 < n_groups:
            lru_project(g + 2)
        elif g + 2 == n_groups:
            sc_b = proj(sc_c0, d_sc)
            sc_c = proj(sc_c0 + d_sc, d_sc)
        else:
            sc_x = proj(sc_c0 + 2 * d_sc, d_sc)
        y = lru_recurrence(g)
        acc = acc + out_project(y, g * gdim)
        if g + 1 < n_groups:
            lru_gates(g + 1)

    v_ext[sc_halo:, :] = sc_c * sc_x
    y_sc = sc_b * _causal_conv(v_ext, scw_ref, 0, d_sc, rows, batch, SC_CONV_WIDTH)
    v_ext[0:sc_halo, :] = v_ext[rows:rows + sc_halo, :]
    o_ref[...] = acc + out_project(y_sc.astype(jnp.bfloat16), d_lru)


def _store_w_in_chunk(step, chunk_ref, w_scr, d_lru):
    gdim = V7X_MXU_DIM
    half = chunk_ref.shape[0] // BF16_PER_WORD
    r0 = pl.multiple_of(step * half, half)
    packed = pltpu.bitcast(chunk_ref[...].astype(jnp.bfloat16), jnp.uint32)
    for g in range(d_lru // gdim):
        w_scr[pl.ds(r0, half), 2 * g * gdim:(2 * g + 1) * gdim] = packed[:, g * gdim:(g + 1) * gdim]
        w_scr[pl.ds(r0, half), (2 * g + 1) * gdim:(2 * g + 2) * gdim] = packed[:, d_lru + g * gdim:d_lru + (g + 1) * gdim]
    w_scr[pl.ds(r0, half), 2 * d_lru:] = packed[:, 2 * d_lru:]


def _mixer_kernel(x_ref, g_ref, w_in_chunk, cw_ref, cb_ref, wg_chunk, ba_ref, bx_ref, lam_ref,
                  scw_ref, w_out_chunk, o_ref,
                  w_in_s, wg_s, w_out_s, zx_ext, v_ext, a_s, b_s, h_carry, *, batch, d_lru, d_sc):
    step = pl.program_id(0)

    @pl.when(step == 0)
    def _():
        zx_ext[0:(LRU_CONV_WIDTH - 1) * batch, :] = jnp.zeros(((LRU_CONV_WIDTH - 1) * batch, d_lru), jnp.float32)
        v_ext[0:(SC_CONV_WIDTH - 1) * batch, :] = jnp.zeros(((SC_CONV_WIDTH - 1) * batch, d_sc), jnp.float32)
        h_carry[...] = jnp.zeros_like(h_carry)

    @pl.when(step < MIXER_WEIGHT_STEPS)
    def _():
        _store_w_in_chunk(step, w_in_chunk, w_in_s, d_lru)
        _store_weight_chunk(step, wg_chunk, wg_s)
        _store_weight_chunk(step, w_out_chunk, w_out_s)

    @pl.when(step >= MIXER_WEIGHT_STEPS)
    def _():
        _mixer_rows(x_ref, g_ref, cw_ref, cb_ref, ba_ref, bx_ref, lam_ref, scw_ref, o_ref,
                    w_in_s, wg_s, w_out_s, zx_ext, v_ext, a_s, b_s, h_carry,
                    batch=batch, d_lru=d_lru, d_sc=d_sc)


def _ffn_rows(x_ref, g_ref, cw_ref, fg_ref, o_ref, w_up_s, w_down_s, u_ext, *, batch, d_ff, final_norm):
    rows, d_model = x_ref.shape
    halo = (FFN_CONV_WIDTH - 1) * batch
    n_chunks = d_ff // FFN_CHUNK

    x = x_ref[...]
    hb = _rms_norm(x, g_ref[...]).astype(jnp.bfloat16)

    def chunk_cols(c):
        return (c * FFN_CHUNK, d_ff + c * FFN_CHUNK)

    def up_project(c):
        for c0 in chunk_cols(c):
            u_ext[halo:, c0:c0 + FFN_CHUNK] = jnp.dot(
                hb, _weight(w_up_s, 0, d_model, c0, FFN_CHUNK), preferred_element_type=jnp.float32)

    def conv(c0):
        y = _causal_conv(u_ext, cw_ref, c0, FFN_CHUNK, rows, batch, FFN_CONV_WIDTH)
        u_ext[0:halo, c0:c0 + FFN_CHUNK] = u_ext[rows:rows + halo, c0:c0 + FFN_CHUNK]
        return y

    acc = x
    up_project(0)
    for c in range(n_chunks):
        if c + 1 < n_chunks:
            up_project(c + 1)
        gate_c0, up_c0 = chunk_cols(c)
        act = (_gelu_tanh(conv(gate_c0)) * conv(up_c0)).astype(jnp.bfloat16)
        acc = acc + jnp.dot(act, _weight(w_down_s, c * FFN_CHUNK, FFN_CHUNK, 0, d_model),
                            preferred_element_type=jnp.float32)
    if final_norm:
        acc = _rms_norm(acc, fg_ref[...])
    o_ref[...] = acc if o_ref.ndim == 2 else _batch_major(acc, batch)


def _ffn_kernel(x_ref, g_ref, w_up_chunk, cw_ref, w_down_chunk, fg_ref, o_ref,
                w_up_s, w_down_s, u_ext, *, batch, d_ff, final_norm):
    step = pl.program_id(0)

    @pl.when(step == 0)
    def _():
        u_ext[0:(FFN_CONV_WIDTH - 1) * batch, :] = jnp.zeros(((FFN_CONV_WIDTH - 1) * batch, 2 * d_ff), jnp.float32)

    @pl.when(step < FFN_WEIGHT_STEPS)
    def _():
        _store_weight_chunk(step, w_up_chunk, w_up_s)
        _store_weight_chunk(step, w_down_chunk, w_down_s)

    @pl.when(step >= FFN_WEIGHT_STEPS)
    def _():
        _ffn_rows(x_ref, g_ref, cw_ref, fg_ref, o_ref, w_up_s, w_down_s, u_ext,
                  batch=batch, d_ff=d_ff, final_norm=final_norm)


def _resident(stacked, layer):
    _, a, b = stacked.shape
    return pl.BlockSpec((None, a, b), lambda i: (layer, 0, 0), pipeline_mode=pl.Buffered(1))


def _weight_chunks(stacked, layer, n_steps):
    _, k, n = stacked.shape
    return pl.BlockSpec((None, k // n_steps, n), lambda i: (layer, jnp.minimum(i, n_steps - 1), 0))


def _packed_weight_scratch(stacked):
    _, k, n = stacked.shape
    return pltpu.VMEM((k // BF16_PER_WORD, n), jnp.uint32)


def _row_spec(rows, d_model, weight_steps):
    return pl.BlockSpec((rows, d_model), lambda i: (jnp.maximum(i - weight_steps, 0), 0))


def _batch_major_spec(rows, batch, d_model, weight_steps):
    return pl.BlockSpec((batch, rows // batch, d_model), lambda i: (0, jnp.maximum(i - weight_steps, 0), 0))


def _compiler_params():
    return pltpu.CompilerParams(dimension_semantics=("arbitrary",),
                                vmem_limit_bytes=VMEM_LIMIT_BYTES)


def _mixer_call(x, layer, g, w_in, cw, cb, wg, ba, bx, lam, scw, w_out, *, batch):
    d_model = x.shape[-1]
    n_rows = x.size // d_model
    d_lru = cw.shape[-1]
    d_sc = scw.shape[-1]
    rows = MIXER_ROWS
    ws = MIXER_WEIGHT_STEPS
    row_spec = _row_spec(rows, d_model, ws)
    res = lambda a: _resident(a, layer)
    chunks = lambda a: _weight_chunks(a, layer, ws)
    return pl.pallas_call(
        functools.partial(_mixer_kernel, batch=batch, d_lru=d_lru, d_sc=d_sc),
        grid=(ws + n_rows // rows,),
        in_specs=[row_spec if x.ndim == 2 else _batch_major_spec(rows, batch, d_model, ws),
                  res(g), chunks(w_in), res(cw), res(cb), chunks(wg), res(ba), res(bx), res(lam),
                  res(scw), chunks(w_out)],
        out_specs=row_spec,
        out_shape=jax.ShapeDtypeStruct((n_rows, d_model), x.dtype),
        scratch_shapes=[
            _packed_weight_scratch(w_in), _packed_weight_scratch(wg), _packed_weight_scratch(w_out),
            pltpu.VMEM((rows + (LRU_CONV_WIDTH - 1) * batch, d_lru), jnp.float32),
            pltpu.VMEM((rows + (SC_CONV_WIDTH - 1) * batch, d_sc), jnp.float32),
            pltpu.VMEM((rows, d_lru), jnp.float32),
            pltpu.VMEM((rows, d_lru), jnp.float32),
            pltpu.VMEM((batch, d_lru), jnp.float32),
        ],
        compiler_params=_compiler_params(),
        name="mixer",
    )(x, g, w_in, cw, cb, wg, ba, bx, lam, scw, w_out)


def _ffn_call(xt, layer, g, w_up, cw, w_down, fg, *, batch, final_norm, batch_major_out):
    n_rows, d_model = xt.shape
    d_ff = cw.shape[-1] // 2
    rows = FFN_ROWS
    ws = FFN_WEIGHT_STEPS
    row_spec = _row_spec(rows, d_model, ws)
    out_shape = (batch, n_rows // batch, d_model) if batch_major_out else (n_rows, d_model)
    return pl.pallas_call(
        functools.partial(_ffn_kernel, batch=batch, d_ff=d_ff, final_norm=final_norm),
        grid=(ws + n_rows // rows,),
        in_specs=[row_spec, _resident(g, layer), _weight_chunks(w_up, layer, ws), _resident(cw, layer),
                  _weight_chunks(w_down, layer, ws), _resident(fg, 0)],
        out_specs=_batch_major_spec(rows, batch, d_model, ws) if batch_major_out else row_spec,
        out_shape=jax.ShapeDtypeStruct(out_shape, xt.dtype),
        scratch_shapes=[_packed_weight_scratch(w_up), _packed_weight_scratch(w_down),
                        pltpu.VMEM((rows + (FFN_CONV_WIDTH - 1) * batch, 2 * d_ff), jnp.float32)],
        compiler_params=_compiler_params(),
        name="convffn",
    )(xt, g, w_up, cw, w_down, fg)


def _block_diag_gates(wa, wx):
    layers, heads, hd, _ = wa.shape
    per = V7X_MXU_DIM // hd
    eye = jnp.eye(per, dtype=wa.dtype)

    def bd(w):
        w = w.reshape(layers, heads // per, per, hd, hd)
        return jnp.einsum('lgjio,jk->lgjiko', w, eye).reshape(layers, heads * hd, V7X_MXU_DIM)

    return jnp.concatenate([bd(wa), bd(wx)], axis=-1)


def kernel(x, norm1_g, w_in, lru_conv_w, lru_conv_b, lru_wa, lru_ba, lru_wx, lru_bx, lru_lambda,
           sc_conv_w, w_out, norm2_g, w_up, ffn_conv_w, w_down, final_g):
    bsz, seq, d_model = x.shape
    depth = w_in.shape[0]
    assert bsz == V7X_SUBLANES and (seq * bsz) % MIXER_ROWS == 0 and (seq * bsz) % FFN_ROWS == 0
    rows_of = lambda v: v.reshape(v.shape[0], 1, v.shape[-1])
    mixer_params = (rows_of(norm1_g), w_in, lru_conv_w, rows_of(lru_conv_b),
                    _block_diag_gates(lru_wa, lru_wx), rows_of(lru_ba), rows_of(lru_bx),
                    rows_of(lru_lambda), sc_conv_w, w_out)
    ffn_params = (rows_of(norm2_g), w_up, ffn_conv_w, w_down, final_g.reshape(1, 1, d_model))
    for l in range(depth):
        last = l == depth - 1
        x = _mixer_call(x, l, *mixer_params, batch=bsz)
        x = _ffn_call(x, l, *ffn_params, batch=bsz, final_norm=last, batch_major_out=last)
    return x
```
